```python
import jax
import jax.numpy as jnp
from jax import lax
import numpy as np

D_MODEL = 1024
BATCH = 32
SEQ = 2048
DEPTH = 2
DEC_BATCH = 128
DEC_SEQ = 1
PAST_LEN = 16384
PAGE_SIZE = 128

D_MIX = D_MODEL
H_M = 4
D_MLSTM = D_MODEL // 4
DK_M = D_MLSTM // H_M
DV_M = D_MLSTM // H_M
D_QK_M = H_M * DK_M
MLSTM_CHUNK = 64
NEG_BIG = -1e30
H_A = 8
D_MLA = D_MODEL // 2
V_A = D_MLA // H_A
NOPE = 64
ROPE_DIM = 32
Q_LORA = 384
KV_LORA = 256
ROPE_THETA = 10000.0
ATTN_SCALE = (NOPE + ROPE_DIM) ** -0.5
ATTN_BLOCK = 128
N_POOL = 4
D_POOL = D_MODEL // 4
D_POOL_G = D_POOL // N_POOL
POOL_WINDOWS = (2, 4, 8, 16)
POOL_BUF = 15
RMS_EPS = 1e-6
SPLIT_SIZES = (D_QK_M, D_QK_M, D_MLSTM, H_M, H_M, D_MLSTM, D_MLSTM, Q_LORA, KV_LORA, ROPE_DIM, D_MLA, D_POOL, D_POOL)
D_IN = 2 * D_QK_M + 3 * D_MLSTM + 2 * H_M + Q_LORA + KV_LORA + ROPE_DIM + D_MLA + 2 * D_POOL

kernel_name = "hymba_mlstm_mla_pool_decode_step"


def rmsnorm(x, w):
    xf = x.astype(jnp.float32)
    y = xf * lax.rsqrt(jnp.mean(xf * xf, -1, keepdims=True) + RMS_EPS)
    return (y * w.astype(jnp.float32)).astype(x.dtype)


def split_columns(p):
    out, off = [], 0
    for s in SPLIT_SIZES:
        out.append(p[..., off:off + s])
        off += s
    return out


def rope(x, pos):
    half = ROPE_DIM // 2
    inv = ROPE_THETA ** (-jnp.arange(half, dtype=jnp.float32) / half)
    ang = pos.astype(jnp.float32)[:, None] * inv[None, :]
    cos, sin = jnp.cos(ang), jnp.sin(ang)
    if x.ndim == 4:
        cos, sin = cos[:, None, :], sin[:, None, :]
    xf = x.astype(jnp.float32)
    x1, x2 = xf[..., :half], xf[..., half:]
    return jnp.concatenate([x1 * cos - x2 * sin, x1 * sin + x2 * cos], -1).astype(x.dtype)


def mlstm_chunkwise(q, k, v, log_i, log_f, state):
    f32 = jnp.float32
    B, L, H, _ = q.shape
    lc = min(MLSTM_CHUNK, L)
    nc = -(-L // lc)
    pad = nc * lc - L
    q, k, v = q.astype(f32), k.astype(f32), v.astype(f32)
    if pad:
        q = jnp.pad(q, ((0, 0), (0, pad), (0, 0), (0, 0)))
        k = jnp.pad(k, ((0, 0), (0, pad), (0, 0), (0, 0)))
        v = jnp.pad(v, ((0, 0), (0, pad), (0, 0), (0, 0)))
        log_i = jnp.pad(log_i, ((0, 0), (0, pad), (0, 0)), constant_values=NEG_BIG)
        log_f = jnp.pad(log_f, ((0, 0), (0, pad), (0, 0)))

    def to_chunks(a):
        a = a.reshape((B, nc, lc) + a.shape[2:])
        return jnp.swapaxes(jnp.moveaxis(a, 1, 0), 2, 3)

    causal = jnp.tril(jnp.ones((lc, lc), dtype=bool))

    def step(carry, xs):
        C, n, m = carry
        qc, kc, vc, li, lf = xs
        b = jnp.cumsum(lf, -1)
        D = jnp.where(causal, b[..., :, None] - b[..., None, :] + li[..., None, :], -jnp.inf)
        inter = b + m[..., None]
        m_t = jnp.maximum(inter, D.max(-1))
        w_intra = jnp.exp(D - m_t[..., None])
        w_inter = jnp.exp(inter - m_t)
        s = jnp.einsum('bhtd,bhsd->bhts', qc, kc) * w_intra
        num = w_inter[..., None] * jnp.einsum('bhtd,bhde->bhte', qc, C) + jnp.einsum('bhts,bhse->bhte', s, vc)
        den = w_inter * jnp.einsum('bhtd,bhd->bht', qc, n) + s.sum(-1)
        h = num / jnp.maximum(jnp.abs(den), jnp.exp(-m_t))[..., None]
        b_last = b[..., -1]
        g = b_last[..., None] - b + li
        m_new = jnp.maximum(b_last + m, g.max(-1))
        decay = jnp.exp(b_last + m - m_new)
        wk = kc * jnp.exp(g - m_new[..., None])[..., None]
        C_new = decay[..., None, None] * C + jnp.einsum('bhsd,bhse->bhde', wk, vc)
        n_new = decay[..., None] * n + wk.sum(-2)
        return (C_new, n_new, m_new), h

    C0, n0, m0 = (a.astype(f32) for a in state)
    xs = (to_chunks(q), to_chunks(k), to_chunks(v), to_chunks(log_i), to_chunks(log_f))
    (C, n, m), h = lax.scan(step, (C0, n0, m0), xs)
    h = jnp.moveaxis(jnp.swapaxes(h, 2, 3), 0, 1).reshape(B, nc * lc, H, DV_M)[:, :L]
    return h, (C, n, m)


def mlstm_branch(q, k, v, ig, fg, og, z, gate_b, norm_w, state):
    B, L, _ = q.shape
    f32 = jnp.float32
    pre = jnp.concatenate([ig, fg], -1).astype(f32) + gate_b.astype(f32)
    log_i = pre[..., :H_M]
    log_f = jax.nn.log_sigmoid(pre[..., H_M:])
    h, new_state = mlstm_chunkwise(q.reshape(B, L, H_M, DK_M), k.reshape(B, L, H_M, DK_M) * (DK_M ** -0.5),
                                   v.reshape(B, L, H_M, DV_M), log_i, log_f, state)
    h = h * jax.nn.sigmoid(og.astype(f32)).reshape(B, L, H_M, DV_M)
    h = rmsnorm(h, norm_w.reshape(H_M, DV_M)).reshape(B, L, D_MLSTM)
    return h.astype(z.dtype) * jax.nn.silu(z), new_state


def mla_branch(cq, ckv_raw, kpe_raw, z, q_norm_w, kv_norm_w, w_uq, w_uk, w_uv, past, start_pos):
    f32 = jnp.float32
    B, L, _ = cq.shape
    pos = start_pos + jnp.arange(L)
    c_q = rmsnorm(cq, q_norm_w)
    q = jnp.einsum('blc,chd->blhd', c_q, w_uq)
    q_nope = q[..., :NOPE]
    q_pe = rope(q[..., NOPE:], pos)
    c_kv = rmsnorm(ckv_raw, kv_norm_w)
    k_pe = rope(kpe_raw, pos)
    if past is None:
        k_nope = jnp.einsum('blc,chd->blhd', c_kv, w_uk)
        v = jnp.einsum('blc,chd->blhd', c_kv, w_uv)
        blk = min(ATTN_BLOCK, L)
        nb = L // blk

        def attend(args):
            qn, qp, qpos = args
            s = jnp.einsum('bqhd,bkhd->bhqk', qn, k_nope) + jnp.einsum('bqhr,bkr->bhqk', qp, k_pe)
            s = jnp.where(pos[None, :] <= qpos[:, None], s.astype(f32) * ATTN_SCALE, -jnp.inf)
            p = jax.nn.softmax(s, -1).astype(v.dtype)
            return jnp.einsum('bhqk,bkhd->bqhd', p, v)

        qn_b = jnp.moveaxis(q_nope.reshape(B, nb, blk, H_A, NOPE), 1, 0)
        qp_b = jnp.moveaxis(q_pe.reshape(B, nb, blk, H_A, ROPE_DIM), 1, 0)
        o = lax.map(attend, (qn_b, qp_b, pos.reshape(nb, blk)))
        o = jnp.moveaxis(o, 0, 1).reshape(B, L, D_MLA)
    else:
        ckv_past, kpe_past = past
        n_past = ckv_past.shape[1]
        q_lat = jnp.einsum('blhd,chd->blhc', q_nope, w_uk)
        s_past = jnp.einsum('blhc,bkc->bhlk', q_lat, ckv_past) + jnp.einsum('blhr,bkr->bhlk', q_pe, kpe_past)
        s_new = jnp.einsum('blhc,bkc->bhlk', q_lat, c_kv) + jnp.einsum('blhr,bkr->bhlk', q_pe, k_pe)
        causal = jnp.arange(L)[None, :] <= jnp.arange(L)[:, None]
        s_new = jnp.where(causal, s_new.astype(f32) * ATTN_SCALE, -jnp.inf)
        s = jnp.concatenate([s_past.astype(f32) * ATTN_SCALE, s_new], -1)
        p = jax.nn.softmax(s, -1).astype(c_kv.dtype)
        o_lat = (jnp.einsum('bhlk,bkc->blhc', p[..., :n_past], ckv_past)
                 + jnp.einsum('bhlk,bkc->blhc', p[..., n_past:], c_kv))
        o = jnp.einsum('blhc,chd->blhd', o_lat, w_uv).reshape(B, L, D_MLA)
    return o * jax.nn.silu(z), c_kv, k_pe


def pool_branch(u, z, prev, pool_w, pool_scale):
    f32 = jnp.float32
    B, L, _ = u.shape
    uf = u.astype(f32)
    ext = uf if prev is None else jnp.concatenate([prev.astype(f32), uf], 1)
    n_prev = ext.shape[1] - L
    cs = jnp.concatenate([jnp.zeros((B, 1, D_POOL), f32), jnp.cumsum(ext, 1)], 1)
    t = n_prev + jnp.arange(L)
    cs_hi = cs[:, t + 1]
    outs = []
    for g, w in enumerate(POOL_WINDOWS):
        lo = jnp.maximum(t + 1 - w, 0)
        cnt = (t + 1 - lo).astype(f32)
        sl = slice(g * D_POOL_G, (g + 1) * D_POOL_G)
        mean = (cs_hi[..., sl] - cs[:, lo, sl]) / cnt[None, :, None]
        outs.append(mean - uf[..., sl])
    d = jnp.stack(outs, 2)
    y = jnp.einsum('blgc,gcd->blgd', d, pool_w.astype(f32)).reshape(B, L, D_POOL) * pool_scale.astype(f32)
    new_buf = ext[:, ext.shape[1] - POOL_BUF:]
    return y.astype(z.dtype) * jax.nn.silu(z), new_buf


def decoder_layer(x, lw, mlstm_state, pool_prev, mla_past, start_pos):
    (pre_w, post_w, w_in, gate_b, mnorm_w, qn_w, kvn_w, w_uq, w_uk, w_uv, pool_w, pool_scale, w_out) = lw
    h = rmsnorm(x, pre_w)
    (q_m, k_m, v_m, ig, fg, og, z_m, cq, ckv_raw, kpe_raw, z_a, u_p, z_p) = split_columns(h @ w_in)
    y_m, new_m = mlstm_branch(q_m, k_m, v_m, ig, fg, og, z_m, gate_b, mnorm_w, mlstm_state)
    y_a, ckv_rows, kpe_rows = mla_branch(cq, ckv_raw, kpe_raw, z_a, qn_w, kvn_w, w_uq, w_uk, w_uv, mla_past, start_pos)
    y_p, pool_buf = pool_branch(u_p, z_p, pool_prev, pool_w, pool_scale)
    mix = jnp.concatenate([y_m, y_a, y_p], -1) @ w_out
    return x + rmsnorm(mix, post_w), new_m, ckv_rows, kpe_rows, pool_buf


def setup_inputs(seed: int = 0) -> dict:
    key = jax.random.key(seed)
    ks = jax.random.split(key, 24)
    f32 = jnp.float32
    n_pages = PAST_LEN // PAGE_SIZE
    n_phys = (DEC_BATCH * n_pages * 5) // 4

    def nrm(k, shape, s=1.0):
        return s * jax.random.normal(k, shape, f32)

    page_table = jax.random.permutation(ks[8], n_phys)[:DEC_BATCH * n_pages]
    page_table = page_table.reshape(DEC_BATCH, n_pages).astype(jnp.int32)
    gate_b = jnp.concatenate([nrm(ks[12], (DEPTH, H_M), 0.1), 3.0 + nrm(ks[13], (DEPTH, H_M), 0.5)], -1)
    return {
        "x_prompt": nrm(ks[0], (BATCH, SEQ, D_MODEL)),
        "x_sample": nrm(ks[1], (DEC_BATCH, DEC_SEQ, D_MODEL)),
        "state_mlstm_C": nrm(ks[2], (DEPTH, DEC_BATCH, H_M, DK_M, DV_M), 0.5),
        "state_mlstm_n": nrm(ks[3], (DEPTH, DEC_BATCH, H_M, DK_M), 0.5),
        "state_mlstm_m": nrm(ks[4], (DEPTH, DEC_BATCH, H_M)),
        "cache_ckv": nrm(ks[5], (DEPTH, n_phys, PAGE_SIZE, KV_LORA)),
        "cache_kpe": nrm(ks[6], (DEPTH, n_phys, PAGE_SIZE, ROPE_DIM)),
        "state_pool": nrm(ks[7], (DEPTH, DEC_BATCH, POOL_BUF, D_POOL)),
        "page_table": page_table,
        "norm_pre_w": 1.0 + nrm(ks[9], (DEPTH, D_MODEL), 0.02),
        "norm_post_w": 1.0 + nrm(ks[10], (DEPTH, D_MODEL), 0.02),
        "w_in": nrm(ks[11], (DEPTH, D_MODEL, D_IN), D_MODEL ** -0.5),
        "mlstm_gate_b": gate_b,
        "mlstm_norm_w": 1.0 + nrm(ks[14], (DEPTH, D_MLSTM), 0.02),
        "mla_q_norm_w": 1.0 + nrm(ks[15], (DEPTH, Q_LORA), 0.02),
        "mla_kv_norm_w": 1.0 + nrm(ks[16], (DEPTH, KV_LORA), 0.02),
        "mla_w_uq": nrm(ks[17], (DEPTH, Q_LORA, H_A, NOPE + ROPE_DIM), Q_LORA ** -0.5),
        "mla_w_uk": nrm(ks[18], (DEPTH, KV_LORA, H_A, NOPE), KV_LORA ** -0.5),
        "mla_w_uv": nrm(ks[19], (DEPTH, KV_LORA, H_A, V_A), KV_LORA ** -0.5),
        "pool_w": nrm(ks[20], (DEPTH, N_POOL, D_POOL_G, D_POOL_G), D_POOL_G ** -0.5),
        "pool_scale": 1.0 + nrm(ks[21], (DEPTH, D_POOL), 0.1),
        "w_out": nrm(ks[22], (DEPTH, D_MIX, D_MODEL), D_MIX ** -0.5),
    }


def reference(x_prompt, x_sample, state_mlstm_C, state_mlstm_n, state_mlstm_m, cache_ckv, cache_kpe,
              state_pool, page_table, norm_pre_w, norm_post_w, w_in, mlstm_gate_b, mlstm_norm_w,
              mla_q_norm_w, mla_kv_norm_w, mla_w_uq, mla_w_uk, mla_w_uv, pool_w, pool_scale, w_out):
    f32 = jnp.float32
    B = x_prompt.shape[0]
    Bd = x_sample.shape[0]
    n_past = page_table.shape[1] * PAGE_SIZE
    yp, ys = x_prompt, x_sample
    pC, pn, pm, pckv, pkpe, ppool = [], [], [], [], [], []
    sC, sn, sm, sckv, skpe, spool = [], [], [], [], [], []
    for l in range(DEPTH):
        lw = (norm_pre_w[l], norm_post_w[l], w_in[l], mlstm_gate_b[l], mlstm_norm_w[l], mla_q_norm_w[l],
              mla_kv_norm_w[l], mla_w_uq[l], mla_w_uk[l], mla_w_uv[l], pool_w[l], pool_scale[l], w_out[l])
        zero_state = (jnp.zeros((B, H_M, DK_M, DV_M), f32), jnp.zeros((B, H_M, DK_M), f32), jnp.zeros((B, H_M), f32))
        yp, (c_, n_, m_), ckv_r, kpe_r, pool_b = decoder_layer(yp, lw, zero_state, None, None, 0)
        pC.append(c_); pn.append(n_); pm.append(m_); pckv.append(ckv_r); pkpe.append(kpe_r); ppool.append(pool_b)
        ckv_past = cache_ckv[l][page_table].reshape(Bd, n_past, KV_LORA)
        kpe_past = cache_kpe[l][page_table].reshape(Bd, n_past, ROPE_DIM)
        ys, (c_, n_, m_), ckv_r, kpe_r, pool_b = decoder_layer(
            ys, lw, (state_mlstm_C[l], state_mlstm_n[l], state_mlstm_m[l]), state_pool[l],
            (ckv_past, kpe_past), n_past)
        sC.append(c_); sn.append(n_); sm.append(m_); sckv.append(ckv_r); skpe.append(kpe_r); spool.append(pool_b)
    return (yp, ys,
            jnp.stack(pC), jnp.stack(pn), jnp.stack(pm), jnp.stack(pckv), jnp.stack(pkpe), jnp.stack(ppool),
            jnp.stack(sC), jnp.stack(sn), jnp.stack(sm), jnp.stack(sckv), jnp.stack(skpe), jnp.stack(spool))
```

```python
import functools

import jax
import jax.numpy as jnp
import numpy as np
from jax import lax
from jax.experimental import pallas as pl
from jax.experimental.pallas import tpu as pltpu

F32 = jnp.float32
BF16 = jnp.bfloat16

D_MODEL = 1024
H_M = 4
D_MLSTM = 256
DK_M = 64
H_A = 8
D_MLA = 512
V_A = 64
NOPE = 64
ROPE_DIM = 32
ROPE_HALF = ROPE_DIM // 2
Q_LORA = 384
KV_LORA = 256
ROPE_THETA = 10000.0
ATTN_SCALE = (NOPE + ROPE_DIM) ** -0.5
D_POOL = 256
D_POOL_G = 64
POOL_WINDOWS = (2, 4, 8, 16)
POOL_BUF = 15
PAGE_SIZE = 128
RMS_EPS = 1e-6
NEG = -1e30

LANES = 128
HEAD_PAD = LANES
PE_OFF = NOPE
ONES_LANE = V_A
VMEM_LIMIT = 56 * 1024 * 1024

MLSTM_CHUNK = 128
ATTN_TQ = 256
ATTN_TK = 256
DECODE_PAGES = 8


def _cparams(sem):
    return pltpu.CompilerParams(dimension_semantics=sem, vmem_limit_bytes=VMEM_LIMIT)


def _rms(x, w):
    return x * lax.rsqrt(jnp.mean(x * x, axis=-1, keepdims=True) + RMS_EPS) * w


def _silu(z):
    return z * jax.nn.sigmoid(z)


def _rope_lanes(x, c, sa, sb):
    return x * c + pltpu.roll(x, LANES - ROPE_HALF, 1) * sa + pltpu.roll(x, ROPE_HALF, 1) * sb


def _shift_rows(x, k, row):
    return jnp.where(row >= k, pltpu.roll(x, k, 0), 0.0)


def _cumsum_rows(x, row):
    k = 1
    while k < x.shape[0]:
        x = x + _shift_rows(x, k, row)
        k *= 2
    return x


def _dot(a, b):
    return jnp.dot(a, b, preferred_element_type=F32)


def _dot_nt(a, b):
    return lax.dot_general(a, b, (((1,), (1,)), ((), ())), preferred_element_type=F32)


def _dot_tn(a, b):
    return lax.dot_general(a, b, (((0,), (0,)), ((), ())), preferred_element_type=F32)


C_M = 0
C_Z = 1024
C_CQ = 2048
C_CKV = 2432
C_U = 2688
C_X = 2944
N_IN = 3072
GATE_OFF = 64


def _inproj_kernel(x_ref, prew_ref, w_ref, qnw_ref, kvnw_ref, gb_ref, c_ref, sa_ref, sb_ref,
                   m_ref, z_ref, cq_ref, ckv_ref, u_ref, kpe_ref, g_ref):
    x = x_ref[...]
    h = _rms(x, prew_ref[...]).astype(BF16)
    m_ref[...] = _dot(h, w_ref[:, C_M:C_Z]).astype(BF16)
    z_ref[...] = _dot(h, w_ref[:, C_Z:C_CQ]).astype(BF16)
    cq_ref[...] = _rms(_dot(h, w_ref[:, C_CQ:C_CKV]), qnw_ref[...]).astype(BF16)
    ckv_ref[...] = _rms(_dot(h, w_ref[:, C_CKV:C_U]), kvnw_ref[...])
    u_ref[...] = _dot(h, w_ref[:, C_U:C_X])
    xg = _dot(h, w_ref[:, C_X:N_IN])
    kpe_ref[...] = _rope_lanes(xg, c_ref[...], sa_ref[...], sb_ref[...])[:, :ROPE_DIM]
    pre = xg + gb_ref[...]
    lane = lax.broadcasted_iota(jnp.int32, pre.shape, 1)
    logsig = jnp.minimum(pre, 0.0) - jnp.log1p(jnp.exp(-jnp.abs(pre)))
    g_ref[...] = jnp.where(lane < GATE_OFF + H_M, pre, logsig)


def _inproj(x, prew, w, qnw, kvnw, gb, tabs, tm):
    t = x.shape[0]
    n_tab = tabs[0].shape[0] // tm
    row = lambda i: (i, 0)
    const = lambda i: (0, 0)
    tab = lambda i: (i % n_tab, 0)
    outs = [
        jax.ShapeDtypeStruct((t, 1024), BF16), jax.ShapeDtypeStruct((t, 1024), BF16),
        jax.ShapeDtypeStruct((t, Q_LORA), BF16), jax.ShapeDtypeStruct((t, KV_LORA), F32),
        jax.ShapeDtypeStruct((t, D_POOL), F32), jax.ShapeDtypeStruct((t, ROPE_DIM), F32),
        jax.ShapeDtypeStruct((t, LANES), F32),
    ]
    return pl.pallas_call(
        _inproj_kernel,
        grid=(t // tm,),
        in_specs=[
            pl.BlockSpec((tm, D_MODEL), row), pl.BlockSpec((1, D_MODEL), const),
            pl.BlockSpec((D_MODEL, N_IN), const), pl.BlockSpec((1, Q_LORA), const),
            pl.BlockSpec((1, KV_LORA), const), pl.BlockSpec((1, LANES), const),
            pl.BlockSpec((tm, LANES), tab), pl.BlockSpec((tm, LANES), tab), pl.BlockSpec((tm, LANES), tab),
        ],
        out_specs=[pl.BlockSpec((tm, o.shape[1]), row) for o in outs],
        out_shape=outs,
        compiler_params=_cparams(("parallel",)),
        name="inproj",
    )(x, prew, w, qnw, kvnw, gb, *tabs)


def _mlstm_kernel(m_ref, g_ref, nw_ref, h_ref, c_out, n_out, m_out, c_s, n_s, m_s):
    ci = pl.program_id(1)
    lc = m_ref.shape[0]

    @pl.when(ci == 0)
    def _():
        c_s[...] = jnp.zeros_like(c_s)
        n_s[...] = jnp.zeros_like(n_s)
        m_s[...] = jnp.zeros_like(m_s)

    g = g_ref[...]
    row = lax.broadcasted_iota(jnp.int32, g.shape, 0)
    bcum = _cumsum_rows(g, row)
    gt = g.T
    bt = bcum.T
    r2 = lax.broadcasted_iota(jnp.int32, (lc, lc), 0)
    c2 = lax.broadcasted_iota(jnp.int32, (lc, lc), 1)
    causal = r2 >= c2
    hs = []
    for h in range(H_M):
        li_c = g[:, GATE_OFF + h:GATE_OFF + h + 1]
        b_c = bcum[:, GATE_OFF + H_M + h:GATE_OFF + H_M + h + 1]
        li_r = gt[GATE_OFF + h:GATE_OFF + h + 1, :]
        b_r = bt[GATE_OFF + H_M + h:GATE_OFF + H_M + h + 1, :]
        m_prev = m_s[h:h + 1, 0:1]
        q = m_ref[:, h * DK_M:(h + 1) * DK_M]
        k = m_ref[:, D_MLSTM + h * DK_M:D_MLSTM + (h + 1) * DK_M]
        v = m_ref[:, 2 * D_MLSTM + h * DK_M:2 * D_MLSTM + (h + 1) * DK_M]
        og = m_ref[:, 3 * D_MLSTM + h * DK_M:3 * D_MLSTM + (h + 1) * DK_M].astype(F32)
        c_prev = c_s[h]
        n_prev = n_s[h:h + 1, :]

        d = jnp.where(causal, b_c - b_r + li_r, NEG)
        inter = b_c + m_prev
        m_t = jnp.maximum(inter, jnp.max(d, axis=1, keepdims=True))
        w_intra = jnp.exp(d - m_t)
        w_inter = jnp.exp(inter - m_t)
        s = _dot_nt(q, k) * w_intra
        num = w_inter * _dot(q, c_prev.astype(BF16)) + _dot(s.astype(BF16), v)
        den = w_inter * jnp.sum(q.astype(F32) * n_prev, axis=1, keepdims=True) + jnp.sum(s, axis=1, keepdims=True)
        hh = num / jnp.maximum(jnp.abs(den), jnp.exp(-m_t))
        hh = hh * jax.nn.sigmoid(og)
        hs.append(_rms(hh, nw_ref[:, h * DK_M:(h + 1) * DK_M]))

        b_last = b_c[lc - 1:lc, :]
        gg = b_last - b_c + li_c
        m_new = jnp.maximum(b_last + m_prev, jnp.max(gg, axis=0, keepdims=True))
        decay = jnp.exp(b_last + m_prev - m_new)
        wk = k.astype(F32) * jnp.exp(gg - m_new)
        c_s[h] = decay * c_prev + _dot_tn(wk.astype(BF16), v)
        n_s[h:h + 1, :] = decay * n_prev + jnp.sum(wk, axis=0, keepdims=True)
        m_s[h:h + 1, :] = jnp.broadcast_to(m_new, (1, LANES))
    h_ref[...] = jnp.concatenate(hs, axis=1).astype(BF16)

    @pl.when(ci == pl.num_programs(1) - 1)
    def _():
        c_out[0] = c_s[...]
        n_out[0] = n_s[...]
        m_out[0] = m_s[0:H_M, :]


def _mlstm_prompt(m, g, nw, b, l):
    lc = min(MLSTM_CHUNK, l)
    nc = l // lc
    row = lambda i, j: (i * nc + j, 0)
    outs = [
        jax.ShapeDtypeStruct((b * l, D_MLSTM), BF16),
        jax.ShapeDtypeStruct((b, H_M, DK_M, DK_M), F32),
        jax.ShapeDtypeStruct((b, H_M, DK_M), F32),
        jax.ShapeDtypeStruct((b, H_M, LANES), F32),
    ]
    return pl.pallas_call(
        _mlstm_kernel,
        grid=(b, nc),
        in_specs=[
            pl.BlockSpec((lc, 1024), row), pl.BlockSpec((lc, LANES), row),
            pl.BlockSpec((1, D_MLSTM), lambda i, j: (0, 0)),
        ],
        out_specs=[
            pl.BlockSpec((lc, D_MLSTM), row),
            pl.BlockSpec((1, H_M, DK_M, DK_M), lambda i, j: (i, 0, 0, 0)),
            pl.BlockSpec((1, H_M, DK_M), lambda i, j: (i, 0, 0)),
            pl.BlockSpec((1, H_M, LANES), lambda i, j: (i, 0, 0)),
        ],
        out_shape=outs,
        scratch_shapes=[pltpu.VMEM((H_M, DK_M, DK_M), F32), pltpu.VMEM((H_M, DK_M), F32),
                        pltpu.VMEM((8, LANES), F32)],
        compiler_params=_cparams(("parallel", "arbitrary")),
        name="mlstm_prompt",
    )(m, g, nw)


def _mlstm_step_kernel(m_ref, g_ref, nw_ref, c_ref, n_ref, mi_ref, h_ref, c_out, n_out, m_out):
    nb = m_ref.shape[0]
    r2 = lax.broadcasted_iota(jnp.int32, (DK_M, DK_M), 0)
    c2 = lax.broadcasted_iota(jnp.int32, (DK_M, DK_M), 1)
    eye = (r2 == c2).astype(F32)

    def to_col(r):
        return jnp.sum(eye * r, axis=1, keepdims=True)

    for b in range(nb):
        mrow = m_ref[b].astype(F32)
        grow = g_ref[b]
        nb_ = n_ref[b]
        m_in = mi_ref[b]
        hs, ms = [], []
        for h in range(H_M):
            q = mrow[:, h * DK_M:(h + 1) * DK_M]
            k = mrow[:, D_MLSTM + h * DK_M:D_MLSTM + (h + 1) * DK_M]
            v = mrow[:, 2 * D_MLSTM + h * DK_M:2 * D_MLSTM + (h + 1) * DK_M]
            og = mrow[:, 3 * D_MLSTM + h * DK_M:3 * D_MLSTM + (h + 1) * DK_M]
            li = grow[:, GATE_OFF + h:GATE_OFF + h + 1]
            lf = grow[:, GATE_OFF + H_M + h:GATE_OFF + H_M + h + 1]
            m0 = m_in[:, h:h + 1]
            c_prev = c_ref[b, h]
            n_prev = nb_[h:h + 1, :]
            inter = lf + m0
            m_t = jnp.maximum(inter, li)
            w_intra = jnp.exp(li - m_t)
            w_inter = jnp.exp(inter - m_t)
            s = jnp.sum(q * k, axis=1, keepdims=True) * w_intra
            qc = jnp.sum(to_col(q) * c_prev, axis=0, keepdims=True)
            num = w_inter * qc + s * v
            den = w_inter * jnp.sum(q * n_prev, axis=1, keepdims=True) + s
            hh = num / jnp.maximum(jnp.abs(den), jnp.exp(-m_t))
            hh = hh * jax.nn.sigmoid(og)
            hs.append(_rms(hh, nw_ref[:, h * DK_M:(h + 1) * DK_M]))
            wk = k * w_intra
            c_out[b, h] = w_inter * c_prev + to_col(wk) * v
            n_out[b, h:h + 1, :] = w_inter * n_prev + wk
            ms.append(m_t)
        h_ref[b] = jnp.concatenate(hs, axis=1).astype(BF16)
        m_out[b] = jnp.concatenate(ms, axis=1)


def _mlstm_step(m, g, nw, c, n, mi, nb=8):
    bd = m.shape[0]
    i3 = lambda i: (i, 0, 0)
    outs = [
        jax.ShapeDtypeStruct((bd, 1, D_MLSTM), BF16),
        jax.ShapeDtypeStruct((bd, H_M, DK_M, DK_M), F32),
        jax.ShapeDtypeStruct((bd, H_M, DK_M), F32),
        jax.ShapeDtypeStruct((bd, 1, H_M), F32),
    ]
    return pl.pallas_call(
        _mlstm_step_kernel,
        grid=(bd // nb,),
        in_specs=[
            pl.BlockSpec((nb, 1, 1024), i3), pl.BlockSpec((nb, 1, LANES), i3),
            pl.BlockSpec((1, D_MLSTM), lambda i: (0, 0)),
            pl.BlockSpec((nb, H_M, DK_M, DK_M), lambda i: (i, 0, 0, 0)),
            pl.BlockSpec((nb, H_M, DK_M), i3), pl.BlockSpec((nb, 1, H_M), i3),
        ],
        out_specs=[
            pl.BlockSpec((nb, 1, D_MLSTM), i3),
            pl.BlockSpec((nb, H_M, DK_M, DK_M), lambda i: (i, 0, 0, 0)),
            pl.BlockSpec((nb, H_M, DK_M), i3), pl.BlockSpec((nb, 1, H_M), i3),
        ],
        out_shape=outs,
        compiler_params=_cparams(("parallel",)),
        name="mlstm_step",
    )(m.reshape(bd, 1, 1024), g.reshape(bd, 1, LANES), nw, c, n, mi.reshape(bd, 1, H_M))


def _qproj_kernel(cq_ref, w_ref, c_ref, sa_ref, sb_ref, q_ref, *, scale):
    q = _dot(cq_ref[...], w_ref[...])
    c, sa, sb = c_ref[...], sa_ref[...], sb_ref[...]
    for h in range(H_A):
        sl = slice(h * HEAD_PAD, (h + 1) * HEAD_PAD)
        q_ref[:, sl] = (_rope_lanes(q[:, sl], c, sa, sb) * scale).astype(BF16)


def _qproj(cq, w, tabs, tm, scale):
    t = cq.shape[0]
    n_tab = tabs[0].shape[0] // tm
    row = lambda i: (i, 0)
    tab = lambda i: (i % n_tab, 0)
    return pl.pallas_call(
        functools.partial(_qproj_kernel, scale=scale),
        grid=(t // tm,),
        in_specs=[pl.BlockSpec((tm, Q_LORA), row), pl.BlockSpec((Q_LORA, H_A * HEAD_PAD), lambda i: (0, 0)),
                  pl.BlockSpec((tm, LANES), tab), pl.BlockSpec((tm, LANES), tab), pl.BlockSpec((tm, LANES), tab)],
        out_specs=pl.BlockSpec((tm, H_A * HEAD_PAD), row),
        out_shape=jax.ShapeDtypeStruct((t, H_A * HEAD_PAD), BF16),
        compiler_params=_cparams(("parallel",)),
        name="qproj",
    )(cq, w, *tabs)


def _kvproj_kernel(ckv_ref, kpe_ref, wk_ref, wpe_ref, wv_ref, vb_ref, k_ref, v_ref):
    ckv = ckv_ref[...].astype(BF16)
    k_ref[...] = (_dot(ckv, wk_ref[...]) + _dot(kpe_ref[...].astype(BF16), wpe_ref[...])).astype(BF16)
    v_ref[...] = (_dot(ckv, wv_ref[...]) + vb_ref[...]).astype(BF16)


def _kvproj(ckv, kpe, wk, wpe, wv, vb, tm):
    t = ckv.shape[0]
    row = lambda i: (i, 0)
    const = lambda i: (0, 0)
    n = H_A * HEAD_PAD
    return pl.pallas_call(
        _kvproj_kernel,
        grid=(t // tm,),
        in_specs=[pl.BlockSpec((tm, KV_LORA), row), pl.BlockSpec((tm, ROPE_DIM), row),
                  pl.BlockSpec((KV_LORA, n), const), pl.BlockSpec((ROPE_DIM, n), const),
                  pl.BlockSpec((KV_LORA, n), const), pl.BlockSpec((1, n), const)],
        out_specs=[pl.BlockSpec((tm, n), row), pl.BlockSpec((tm, n), row)],
        out_shape=[jax.ShapeDtypeStruct((t, n), BF16), jax.ShapeDtypeStruct((t, n), BF16)],
        compiler_params=_cparams(("parallel",)),
        name="kvproj",
    )(ckv, kpe, wk, wpe, wv, vb)


def _flash_kernel(q_ref, k_ref, v_ref, o_ref):
    qi = pl.program_id(1)
    tq = q_ref.shape[0]
    tk = tq
    r2 = lax.broadcasted_iota(jnp.int32, (tq, tk), 0)
    c2 = lax.broadcasted_iota(jnp.int32, (tq, tk), 1)
    causal = r2 >= c2

    def block(q, j, sl, carry, masked):
        m, acc = carry
        off = pl.multiple_of(j * tk, tk)
        k = k_ref[pl.ds(off, tk), sl]
        v = v_ref[pl.ds(off, tk), sl]
        s = _dot_nt(q, k)
        if masked:
            s = jnp.where(causal, s, NEG)
        m_new = jnp.maximum(m, jnp.max(s, axis=1, keepdims=True))
        p = jnp.exp(s - m_new)
        acc = jnp.exp(m - m_new) * acc + _dot(p.astype(BF16), v)
        return m_new, acc

    for hp in range(H_A // 2):
        pair = []
        for h in (2 * hp, 2 * hp + 1):
            sl = slice(h * HEAD_PAD, (h + 1) * HEAD_PAD)
            q = q_ref[:, sl]
            init = (jnp.full((tq, 1), NEG, F32), jnp.zeros((tq, HEAD_PAD), F32))
            carry = lax.fori_loop(0, qi, lambda j, c: block(q, j, sl, c, False), init)
            _, acc = block(q, qi, sl, carry, True)
            pair.append(acc[:, :V_A] / acc[:, ONES_LANE:ONES_LANE + 1])
        o_ref[:, hp * LANES:(hp + 1) * LANES] = jnp.concatenate(pair, axis=1).astype(BF16)


def _flash(q, k, v, b, l):
    tq = min(ATTN_TQ, l)
    nq = l // tq
    n = H_A * HEAD_PAD
    return pl.pallas_call(
        _flash_kernel,
        grid=(b, nq),
        in_specs=[pl.BlockSpec((tq, n), lambda i, j: (i * nq + j, 0)),
                  pl.BlockSpec((l, n), lambda i, j: (i, 0)), pl.BlockSpec((l, n), lambda i, j: (i, 0))],
        out_specs=pl.BlockSpec((tq, D_MLA), lambda i, j: (i * nq + j, 0)),
        out_shape=jax.ShapeDtypeStruct((b * l, D_MLA), BF16),
        compiler_params=_cparams(("parallel", "arbitrary")),
        name="flash",
    )(q, k, v)


def _matmul_kernel(x_ref, w_ref, o_ref):
    o_ref[...] = _dot(x_ref[...].astype(BF16), w_ref[...]).astype(o_ref.dtype)


def _matmul(x, w, dtype, name):
    t, kd = x.shape
    n = w.shape[1]
    return pl.pallas_call(
        _matmul_kernel,
        grid=(1,),
        in_specs=[pl.BlockSpec((t, kd), lambda i: (0, 0)), pl.BlockSpec((kd, n), lambda i: (0, 0))],
        out_specs=pl.BlockSpec((t, n), lambda i: (0, 0)),
        out_shape=jax.ShapeDtypeStruct((t, n), dtype),
        compiler_params=_cparams(("arbitrary",)),
        name=name,
    )(x, w)


def _decode_kernel(pt_ref, ql_ref, qp_ref, cn_ref, kn_ref, ckv_hbm, kpe_hbm, o_ref,
                   cbuf, kbuf, csem, ksem, *, layer, n_chunks, g_pages):
    b = pl.program_id(0)
    nb = pl.num_programs(0)

    def copies(bb, c, slot):
        out = []
        for g in range(g_pages):
            page = pt_ref[bb, c * g_pages + g]
            dst = pl.ds(g * PAGE_SIZE, PAGE_SIZE)
            out.append(pltpu.make_async_copy(ckv_hbm.at[layer, page], cbuf.at[slot, dst], csem.at[slot]))
            out.append(pltpu.make_async_copy(kpe_hbm.at[layer, page], kbuf.at[slot, dst], ksem.at[slot]))
        return out

    def start(bb, c, slot):
        for cp in copies(bb, c, slot):
            cp.start()

    def wait(bb, c, slot):
        for cp in copies(bb, c, slot):
            cp.wait()

    @pl.when(b == 0)
    def _():
        start(0, 0, 0)

    ql = ql_ref[0].astype(BF16)
    qp = qp_ref[0].astype(BF16)

    def chunk(c, carry):
        m, l, acc = carry
        flat = b * n_chunks + c
        slot = flat % 2
        nxt_c = jnp.where(c + 1 < n_chunks, c + 1, 0)
        nxt_b = jnp.where(c + 1 < n_chunks, b, b + 1)

        @pl.when(nxt_b < nb)
        def _():
            start(nxt_b, nxt_c, 1 - slot)

        wait(b, c, slot)
        ckv = cbuf[slot].astype(BF16)
        kpe = kbuf[slot].astype(BF16)
        s = (_dot_nt(ql, ckv) + _dot_nt(qp, kpe)) * ATTN_SCALE
        m_new = jnp.maximum(m, jnp.max(s, axis=1, keepdims=True))
        p = jnp.exp(s - m_new)
        alpha = jnp.exp(m - m_new)
        l = alpha * l + jnp.sum(p, axis=1, keepdims=True)
        acc = alpha * acc + _dot(p.astype(BF16), ckv)
        return m_new, l, acc

    init = (jnp.full((H_A, 1), NEG, F32), jnp.zeros((H_A, 1), F32), jnp.zeros((H_A, KV_LORA), F32))
    m, l, acc = lax.fori_loop(0, n_chunks, chunk, init)

    cn = cn_ref[0]
    kn = kn_ref[0]
    s_new = (jnp.sum(ql.astype(F32) * cn.astype(BF16).astype(F32), axis=1, keepdims=True)
             + jnp.sum(qp.astype(F32) * kn.astype(BF16).astype(F32), axis=1, keepdims=True)) * ATTN_SCALE
    m_f = jnp.maximum(m, s_new)
    p_new = jnp.exp(s_new - m_f)
    alpha = jnp.exp(m - m_f)
    l = alpha * l + p_new
    acc = alpha * acc + p_new.astype(BF16).astype(F32) * cn.astype(BF16).astype(F32)
    o_ref[0] = acc / l


def _decode(page_table, ql, qp, cn, kn, cache_ckv, cache_kpe, layer):
    bd, n_pages = page_table.shape
    g_pages = min(DECODE_PAGES, n_pages)
    n_chunks = n_pages // g_pages
    rows = g_pages * PAGE_SIZE
    i3 = lambda i, pt: (i, 0, 0)
    grid_spec = pltpu.PrefetchScalarGridSpec(
        num_scalar_prefetch=1,
        grid=(bd,),
        in_specs=[pl.BlockSpec((1, H_A, KV_LORA), i3), pl.BlockSpec((1, H_A, ROPE_DIM), i3),
                  pl.BlockSpec((1, 1, KV_LORA), i3), pl.BlockSpec((1, 1, ROPE_DIM), i3),
                  pl.BlockSpec(memory_space=pl.ANY), pl.BlockSpec(memory_space=pl.ANY)],
        out_specs=pl.BlockSpec((1, H_A, KV_LORA), i3),
        scratch_shapes=[pltpu.VMEM((2, rows, KV_LORA), F32), pltpu.VMEM((2, rows, ROPE_DIM), F32),
                        pltpu.SemaphoreType.DMA((2,)), pltpu.SemaphoreType.DMA((2,))],
    )
    return pl.pallas_call(
        functools.partial(_decode_kernel, layer=layer, n_chunks=n_chunks, g_pages=g_pages),
        grid_spec=grid_spec,
        out_shape=jax.ShapeDtypeStruct((bd, H_A, KV_LORA), F32),
        compiler_params=_cparams(("arbitrary",)),
        name="decode",
    )(page_table, ql, qp, cn, kn, cache_ckv, cache_kpe)


def _pool_kernel(u_ref, w_ref, sc_ref, y_ref):
    u = u_ref[0]
    row = lax.broadcasted_iota(jnp.int32, u.shape, 0)
    lane = lax.broadcasted_iota(jnp.int32, u.shape, 1)
    cnt = (row + 1).astype(F32)
    acc = u
    mean = jnp.zeros_like(u)
    k = 1
    for gi, w in enumerate(POOL_WINDOWS):
        while k < w:
            acc = acc + _shift_rows(acc, k, row)
            k *= 2
        in_group = (lane >= gi * D_POOL_G) & (lane < (gi + 1) * D_POOL_G)
        mean = jnp.where(in_group, acc / jnp.minimum(cnt, float(w)), mean)
    d = (mean - u).astype(BF16)
    y_ref[0] = (_dot(d, w_ref[...]) * sc_ref[...]).astype(BF16)


def _pool(u, w, sc):
    b, l, _ = u.shape
    i3 = lambda i: (i, 0, 0)
    return pl.pallas_call(
        _pool_kernel,
        grid=(b,),
        in_specs=[pl.BlockSpec((1, l, D_POOL), i3), pl.BlockSpec((D_POOL, D_POOL), lambda i: (0, 0)),
                  pl.BlockSpec((1, D_POOL), lambda i: (0, 0))],
        out_specs=pl.BlockSpec((1, l, D_POOL), i3),
        out_shape=jax.ShapeDtypeStruct((b, l, D_POOL), BF16),
        compiler_params=_cparams(("parallel",)),
        name="pool",
    )(u, w, sc)


def _outproj_kernel(x_ref, hm_ref, oa_ref, yp_ref, z_ref, w_ref, pw_ref, o_ref):
    z = z_ref[...].astype(F32)
    g = _silu(z)
    ym = (hm_ref[...].astype(F32) * g[:, :D_MLSTM]).astype(BF16)
    ya = (oa_ref[...].astype(F32) * g[:, D_MLSTM:D_MLSTM + D_MLA]).astype(BF16)
    yp = (yp_ref[...].astype(F32) * g[:, D_MLSTM + D_MLA:]).astype(BF16)
    mix = (_dot(ym, w_ref[:D_MLSTM, :]) + _dot(ya, w_ref[D_MLSTM:D_MLSTM + D_MLA, :])
           + _dot(yp, w_ref[D_MLSTM + D_MLA:, :]))
    o_ref[...] = x_ref[...] + _rms(mix, pw_ref[...])


def _outproj(x, hm, oa, yp, z, w, pw, tm):
    t = x.shape[0]
    row = lambda i: (i, 0)
    const = lambda i: (0, 0)
    return pl.pallas_call(
        _outproj_kernel,
        grid=(t // tm,),
        in_specs=[pl.BlockSpec((tm, D_MODEL), row), pl.BlockSpec((tm, D_MLSTM), row),
                  pl.BlockSpec((tm, D_MLA), row), pl.BlockSpec((tm, D_POOL), row),
                  pl.BlockSpec((tm, D_MODEL), row), pl.BlockSpec((D_MODEL, D_MODEL), const),
                  pl.BlockSpec((1, D_MODEL), const)],
        out_specs=pl.BlockSpec((tm, D_MODEL), row),
        out_shape=jax.ShapeDtypeStruct((t, D_MODEL), F32),
        compiler_params=_cparams(("parallel",)),
        name="outproj",
    )(x, hm, oa, yp, z, w, pw)


def _rope_tables(pos, off):
    inv = ROPE_THETA ** (-jnp.arange(ROPE_HALF, dtype=F32) / ROPE_HALF)
    ang = pos.astype(F32)[:, None] * inv[None, :]
    cos, sin = jnp.cos(ang), jnp.sin(ang)
    n = pos.shape[0]
    z16 = jnp.zeros((n, ROPE_HALF), F32)
    pad_l = lambda fill: jnp.full((n, off), fill, F32)
    pad_r = jnp.zeros((n, LANES - off - ROPE_DIM), F32)
    c = jnp.concatenate([pad_l(1.0), cos, cos, pad_r], axis=1)
    sa = jnp.concatenate([pad_l(0.0), -sin, z16, pad_r], axis=1)
    sb = jnp.concatenate([pad_l(0.0), z16, sin, pad_r], axis=1)
    return c, sa, sb


def _layer_weights(w_in, gate_b, w_uq, w_uk, w_uv, pool_w, w_out):
    o = np.cumsum([0, 256, 256, 256, 4, 4, 256, 256, 384, 256, 32, 512, 256, 256])
    col = lambda i: w_in[:, o[i]:o[i + 1]]
    q_m, k_m, v_m, ig, fg, og, z_m, cq, ckv, kpe, z_a, u_p, z_p = [col(i) for i in range(13)]
    zc = lambda n: jnp.zeros((D_MODEL, n), F32)
    w_all = jnp.concatenate([
        q_m, k_m * (DK_M ** -0.5), v_m, og, z_m, z_a, z_p, cq, ckv, u_p,
        kpe, zc(GATE_OFF - ROPE_DIM), ig, fg, zc(LANES - GATE_OFF - 2 * H_M)], axis=1).astype(BF16)
    gb = jnp.concatenate([jnp.zeros((GATE_OFF,), F32), gate_b, jnp.zeros((LANES - GATE_OFF - 2 * H_M,), F32)])[None]

    pad_q = jnp.zeros((Q_LORA, H_A, HEAD_PAD - NOPE - ROPE_DIM), F32)
    wq = jnp.concatenate([w_uq, pad_q], axis=2).reshape(Q_LORA, H_A * HEAD_PAD).astype(BF16)
    wk = jnp.concatenate([w_uk, jnp.zeros((KV_LORA, H_A, HEAD_PAD - NOPE), F32)], axis=2)
    wk = wk.reshape(KV_LORA, H_A * HEAD_PAD).astype(BF16)
    place = jnp.concatenate([jnp.zeros((ROPE_DIM, PE_OFF), F32), jnp.eye(ROPE_DIM, dtype=F32),
                             jnp.zeros((ROPE_DIM, HEAD_PAD - PE_OFF - ROPE_DIM), F32)], axis=1)
    wpe = jnp.tile(place, (1, H_A)).astype(BF16)
    wv = jnp.concatenate([w_uv, jnp.zeros((KV_LORA, H_A, HEAD_PAD - V_A), F32)], axis=2)
    wv = wv.reshape(KV_LORA, H_A * HEAD_PAD).astype(BF16)
    vb = jnp.tile((jnp.arange(HEAD_PAD) == ONES_LANE).astype(F32), H_A)[None]

    eye_h = jnp.eye(H_A, dtype=F32)
    wabs = jnp.concatenate([jnp.transpose(w_uk, (1, 2, 0)),
                            jnp.zeros((H_A, HEAD_PAD - NOPE, KV_LORA), F32)], axis=1)
    wabs = jnp.einsum('hdc,hg->hdgc', wabs, eye_h).reshape(H_A * HEAD_PAD, H_A * KV_LORA).astype(BF16)
    wuv_bd = jnp.einsum('chd,hg->hcgd', w_uv, eye_h).reshape(H_A * KV_LORA, D_MLA).astype(BF16)
    eye_g = jnp.eye(len(POOL_WINDOWS), dtype=F32)
    wpool = jnp.einsum('gcd,gk->gckd', pool_w, eye_g).reshape(D_POOL, D_POOL).astype(BF16)
    return dict(w_all=w_all, gb=gb, wq=wq, wk=wk, wpe=wpe, wv=wv, vb=vb, wabs=wabs, wuv_bd=wuv_bd,
                wpool=wpool, w_out=w_out.astype(BF16))


def _row_tile(t):
    return min(512, t)


def kernel(x_prompt, x_sample, state_mlstm_C, state_mlstm_n, state_mlstm_m, cache_ckv, cache_kpe, state_pool, page_table, norm_pre_w, norm_post_w, w_in, mlstm_gate_b, mlstm_norm_w, mla_q_norm_w, mla_kv_norm_w, mla_w_uq, mla_w_uk, mla_w_uv, pool_w, pool_scale, w_out):
    b, l, _ = x_prompt.shape
    bd = x_sample.shape[0]
    depth = w_in.shape[0]
    n_past = page_table.shape[1] * PAGE_SIZE
    tp = b * l
    tm_p, tm_s = _row_tile(min(tp, l)), _row_tile(bd)

    pos_p = jnp.arange(l)
    pos_s = jnp.full((tm_s,), n_past)
    tabs_in_p, tabs_q_p = _rope_tables(pos_p, 0), _rope_tables(pos_p, PE_OFF)
    tabs_in_s, tabs_q_s = _rope_tables(pos_s, 0), _rope_tables(pos_s, PE_OFF)

    yp = x_prompt.reshape(tp, D_MODEL)
    ys = x_sample.reshape(bd, D_MODEL)
    outs = [[] for _ in range(12)]
    for li in range(depth):
        lw = _layer_weights(w_in[li], mlstm_gate_b[li], mla_w_uq[li], mla_w_uk[li], mla_w_uv[li],
                            pool_w[li], w_out[li])
        prew, postw = norm_pre_w[li][None], norm_post_w[li][None]
        qnw, kvnw, mnw = mla_q_norm_w[li][None], mla_kv_norm_w[li][None], mlstm_norm_w[li][None]
        psc = pool_scale[li][None]

        m, z, cq, ckv, u, kpe, g = _inproj(yp, prew, lw['w_all'], qnw, kvnw, lw['gb'], tabs_in_p, tm_p)
        hm, c_f, n_f, m_f = _mlstm_prompt(m, g, mnw, b, l)
        q = _qproj(cq, lw['wq'], tabs_q_p, tm_p, ATTN_SCALE)
        kk, vv = _kvproj(ckv, kpe, lw['wk'], lw['wpe'], lw['wv'], lw['vb'], tm_p)
        oa = _flash(q, kk, vv, b, l)
        u3 = u.reshape(b, l, D_POOL)
        ypool = _pool(u3, lw['wpool'], psc).reshape(tp, D_POOL)
        yp = _outproj(yp, hm, oa, ypool, z, lw['w_out'], postw, tm_p)
        outs[0].append(c_f)
        outs[1].append(n_f)
        outs[2].append(m_f[:, :, 0])
        outs[3].append(ckv.reshape(b, l, KV_LORA))
        outs[4].append(kpe.reshape(b, l, ROPE_DIM))
        outs[5].append(u3[:, l - POOL_BUF:])

        m, z, cq, ckv, u, kpe, g = _inproj(ys, prew, lw['w_all'], qnw, kvnw, lw['gb'], tabs_in_s, tm_s)
        hm, c_n, n_n, m_n = _mlstm_step(m, g, mnw, state_mlstm_C[li], state_mlstm_n[li], state_mlstm_m[li])
        q = _qproj(cq, lw['wq'], tabs_q_s, tm_s, 1.0)
        ql = _matmul(q, lw['wabs'], F32, "absorb").reshape(bd, H_A, KV_LORA)
        qp = q.reshape(bd, H_A, HEAD_PAD)[:, :, PE_OFF:PE_OFF + ROPE_DIM]
        o_lat = _decode(page_table, ql, qp, ckv.reshape(bd, 1, KV_LORA), kpe.reshape(bd, 1, ROPE_DIM),
                        cache_ckv, cache_kpe, li)
        oa = _matmul(o_lat.reshape(bd, H_A * KV_LORA), lw['wuv_bd'], BF16, "uv")
        ext = jnp.concatenate([state_pool[li], u[:, None, :]], axis=1)
        ypool = _pool(ext, lw['wpool'], psc)[:, POOL_BUF]
        ys = _outproj(ys, hm.reshape(bd, D_MLSTM), oa, ypool, z, lw['w_out'], postw, tm_s)
        outs[6].append(c_n)
        outs[7].append(n_n)
        outs[8].append(m_n.reshape(bd, H_M))
        outs[9].append(ckv.reshape(bd, 1, KV_LORA))
        outs[10].append(kpe.reshape(bd, 1, ROPE_DIM))
        outs[11].append(ext[:, 1:])

    return (yp.reshape(b, l, D_MODEL), ys.reshape(bd, 1, D_MODEL)) + tuple(jnp.stack(o) for o in outs)
```

```python
import functools

import jax
import jax.numpy as jnp
import numpy as np
from jax import lax
from jax.experimental import pallas as pl
from jax.experimental.pallas import tpu as pltpu

F32 = jnp.float32
BF16 = jnp.bfloat16

D_MODEL = 1024
H_M = 4
D_MLSTM = 256
DK_M = 64
H_A = 8
D_MLA = 512
V_A = 64
NOPE = 64
ROPE_DIM = 32
ROPE_HALF = ROPE_DIM // 2
Q_LORA = 384
KV_LORA = 256
ROPE_THETA = 10000.0
ATTN_SCALE = (NOPE + ROPE_DIM) ** -0.5
D_POOL = 256
D_POOL_G = 64
POOL_WINDOWS = (2, 4, 8, 16)
POOL_BUF = 15
PAGE_SIZE = 128
RMS_EPS = 1e-6
NEG = -1e30

LANES = 128
HEAD_PAD = LANES
PE_OFF = NOPE
ONES_LANE = V_A
VMEM_LIMIT = 56 * 1024 * 1024

MLSTM_CHUNK = 128
ATTN_TQ = 256
FLASH_HEADS = 8
LOG2E = 1.4426950408889634
DECODE_PAGES = 8
DECODE_SEQS = 4


def _cparams(sem):
    return pltpu.CompilerParams(dimension_semantics=sem, vmem_limit_bytes=VMEM_LIMIT)


def _rms(x, w):
    return x * lax.rsqrt(jnp.mean(x * x, axis=-1, keepdims=True) + RMS_EPS) * w


def _silu(z):
    return z * jax.nn.sigmoid(z)


def _rope_lanes(x, c, sa, sb):
    return x * c + pltpu.roll(x, LANES - ROPE_HALF, 1) * sa + pltpu.roll(x, ROPE_HALF, 1) * sb


def _shift_rows(x, k, row):
    return jnp.where(row >= k, pltpu.roll(x, k, 0), 0.0)


def _cumsum_rows(x, row):
    k = 1
    while k < x.shape[0]:
        x = x + _shift_rows(x, k, row)
        k *= 2
    return x


def _dot(a, b):
    return jnp.dot(a, b, preferred_element_type=F32)


def _dot_nt(a, b):
    return lax.dot_general(a, b, (((1,), (1,)), ((), ())), preferred_element_type=F32)


def _dot_tn(a, b):
    return lax.dot_general(a, b, (((0,), (0,)), ((), ())), preferred_element_type=F32)


C_M = 0
C_Z = 1024
C_CQ = 2048
C_CKV = 2432
C_U = 2688
C_X = 2944
N_IN = 3072
GATE_OFF = 64


def _inproj_kernel(x_ref, prew_ref, w_ref, qnw_ref, kvnw_ref, gb_ref, c_ref, sa_ref, sb_ref,
                   m_ref, z_ref, cq_ref, ckv_ref, u_ref, kpe_ref, g_ref):
    x = x_ref[...]
    h = _rms(x, prew_ref[...]).astype(BF16)
    m_ref[...] = _dot(h, w_ref[:, C_M:C_Z]).astype(BF16)
    z_ref[...] = _dot(h, w_ref[:, C_Z:C_CQ]).astype(BF16)
    cq_ref[...] = _rms(_dot(h, w_ref[:, C_CQ:C_CKV]), qnw_ref[...]).astype(BF16)
    ckv_ref[...] = _rms(_dot(h, w_ref[:, C_CKV:C_U]), kvnw_ref[...])
    u_ref[...] = _dot(h, w_ref[:, C_U:C_X])
    xg = _dot(h, w_ref[:, C_X:N_IN])
    kpe_ref[...] = _rope_lanes(xg, c_ref[...], sa_ref[...], sb_ref[...])[:, :ROPE_DIM]
    pre = xg + gb_ref[...]
    lane = lax.broadcasted_iota(jnp.int32, pre.shape, 1)
    logsig = jnp.minimum(pre, 0.0) - jnp.log1p(jnp.exp(-jnp.abs(pre)))
    g_ref[...] = jnp.where(lane < GATE_OFF + H_M, pre, logsig)


def _inproj(x, prew, w, qnw, kvnw, gb, tabs, tm):
    t = x.shape[0]
    n_tab = tabs[0].shape[0] // tm
    row = lambda i: (i, 0)
    const = lambda i: (0, 0)
    tab = lambda i: (i % n_tab, 0)
    outs = [
        jax.ShapeDtypeStruct((t, 1024), BF16), jax.ShapeDtypeStruct((t, 1024), BF16),
        jax.ShapeDtypeStruct((t, Q_LORA), BF16), jax.ShapeDtypeStruct((t, KV_LORA), F32),
        jax.ShapeDtypeStruct((t, D_POOL), F32), jax.ShapeDtypeStruct((t, ROPE_DIM), F32),
        jax.ShapeDtypeStruct((t, LANES), F32),
    ]
    return pl.pallas_call(
        _inproj_kernel,
        grid=(t // tm,),
        in_specs=[
            pl.BlockSpec((tm, D_MODEL), row), pl.BlockSpec((1, D_MODEL), const),
            pl.BlockSpec((D_MODEL, N_IN), const), pl.BlockSpec((1, Q_LORA), const),
            pl.BlockSpec((1, KV_LORA), const), pl.BlockSpec((1, LANES), const),
            pl.BlockSpec((tm, LANES), tab), pl.BlockSpec((tm, LANES), tab), pl.BlockSpec((tm, LANES), tab),
        ],
        out_specs=[pl.BlockSpec((tm, o.shape[1]), row) for o in outs],
        out_shape=outs,
        compiler_params=_cparams(("parallel",)),
        name="inproj",
    )(x, prew, w, qnw, kvnw, gb, *tabs)


def _mlstm_kernel(m_ref, g_ref, nw_ref, h_ref, c_out, n_out, m_out, c_s, n_s, m_s):
    ci = pl.program_id(1)
    lc = m_ref.shape[0]

    @pl.when(ci == 0)
    def _():
        c_s[...] = jnp.zeros_like(c_s)
        n_s[...] = jnp.zeros_like(n_s)
        m_s[...] = jnp.zeros_like(m_s)

    g = g_ref[...]
    row = lax.broadcasted_iota(jnp.int32, g.shape, 0)
    bcum = _cumsum_rows(g, row)
    gt = g.T
    bt = bcum.T
    r2 = lax.broadcasted_iota(jnp.int32, (lc, lc), 0)
    c2 = lax.broadcasted_iota(jnp.int32, (lc, lc), 1)
    causal = r2 >= c2
    hs = []
    for h in range(H_M):
        li_c = g[:, GATE_OFF + h:GATE_OFF + h + 1]
        b_c = bcum[:, GATE_OFF + H_M + h:GATE_OFF + H_M + h + 1]
        li_r = gt[GATE_OFF + h:GATE_OFF + h + 1, :]
        b_r = bt[GATE_OFF + H_M + h:GATE_OFF + H_M + h + 1, :]
        m_prev = m_s[h:h + 1, 0:1]
        q = m_ref[:, h * DK_M:(h + 1) * DK_M]
        k = m_ref[:, D_MLSTM + h * DK_M:D_MLSTM + (h + 1) * DK_M]
        v = m_ref[:, 2 * D_MLSTM + h * DK_M:2 * D_MLSTM + (h + 1) * DK_M]
        og = m_ref[:, 3 * D_MLSTM + h * DK_M:3 * D_MLSTM + (h + 1) * DK_M].astype(F32)
        c_prev = c_s[h]
        n_prev = n_s[h:h + 1, :]

        d = jnp.where(causal, b_c - b_r + li_r, NEG)
        inter = b_c + m_prev
        m_t = jnp.maximum(inter, jnp.max(d, axis=1, keepdims=True))
        w_intra = jnp.exp(d - m_t)
        w_inter = jnp.exp(inter - m_t)
        s = _dot_nt(q, k) * w_intra
        num = w_inter * _dot(q, c_prev.astype(BF16)) + _dot(s.astype(BF16), v)
        den = w_inter * jnp.sum(q.astype(F32) * n_prev, axis=1, keepdims=True) + jnp.sum(s, axis=1, keepdims=True)
        hh = num / jnp.maximum(jnp.abs(den), jnp.exp(-m_t))
        hh = hh * jax.nn.sigmoid(og)
        hs.append(_rms(hh, nw_ref[:, h * DK_M:(h + 1) * DK_M]))

        b_last = b_c[lc - 1:lc, :]
        gg = b_last - b_c + li_c
        m_new = jnp.maximum(b_last + m_prev, jnp.max(gg, axis=0, keepdims=True))
        decay = jnp.exp(b_last + m_prev - m_new)
        wk = k.astype(F32) * jnp.exp(gg - m_new)
        c_s[h] = decay * c_prev + _dot_tn(wk.astype(BF16), v)
        n_s[h:h + 1, :] = decay * n_prev + jnp.sum(wk, axis=0, keepdims=True)
        m_s[h:h + 1, :] = jnp.broadcast_to(m_new, (1, LANES))
    h_ref[...] = jnp.concatenate(hs, axis=1).astype(BF16)

    @pl.when(ci == pl.num_programs(1) - 1)
    def _():
        c_out[0] = c_s[...]
        n_out[0] = n_s[...]
        m_out[0] = m_s[0:H_M, :]


def _mlstm_prompt(m, g, nw, b, l):
    lc = min(MLSTM_CHUNK, l)
    nc = l // lc
    row = lambda i, j: (i * nc + j, 0)
    outs = [
        jax.ShapeDtypeStruct((b * l, D_MLSTM), BF16),
        jax.ShapeDtypeStruct((b, H_M, DK_M, DK_M), F32),
        jax.ShapeDtypeStruct((b, H_M, DK_M), F32),
        jax.ShapeDtypeStruct((b, H_M, LANES), F32),
    ]
    return pl.pallas_call(
        _mlstm_kernel,
        grid=(b, nc),
        in_specs=[
            pl.BlockSpec((lc, 1024), row), pl.BlockSpec((lc, LANES), row),
            pl.BlockSpec((1, D_MLSTM), lambda i, j: (0, 0)),
        ],
        out_specs=[
            pl.BlockSpec((lc, D_MLSTM), row),
            pl.BlockSpec((1, H_M, DK_M, DK_M), lambda i, j: (i, 0, 0, 0)),
            pl.BlockSpec((1, H_M, DK_M), lambda i, j: (i, 0, 0)),
            pl.BlockSpec((1, H_M, LANES), lambda i, j: (i, 0, 0)),
        ],
        out_shape=outs,
        scratch_shapes=[pltpu.VMEM((H_M, DK_M, DK_M), F32), pltpu.VMEM((H_M, DK_M), F32),
                        pltpu.VMEM((8, LANES), F32)],
        compiler_params=_cparams(("parallel", "arbitrary")),
        name="mlstm_prompt",
    )(m, g, nw)


def _mlstm_step_kernel(m_ref, g_ref, nw_ref, c_ref, n_ref, mi_ref, h_ref, c_out, n_out, m_out):
    nb = m_ref.shape[0]
    r2 = lax.broadcasted_iota(jnp.int32, (DK_M, DK_M), 0)
    c2 = lax.broadcasted_iota(jnp.int32, (DK_M, DK_M), 1)
    eye = (r2 == c2).astype(F32)

    def to_col(r):
        return jnp.sum(eye * r, axis=1, keepdims=True)

    for b in range(nb):
        mrow = m_ref[b].astype(F32)
        grow = g_ref[b]
        nb_ = n_ref[b]
        m_in = mi_ref[b]
        hs, ms = [], []
        for h in range(H_M):
            q = mrow[:, h * DK_M:(h + 1) * DK_M]
            k = mrow[:, D_MLSTM + h * DK_M:D_MLSTM + (h + 1) * DK_M]
            v = mrow[:, 2 * D_MLSTM + h * DK_M:2 * D_MLSTM + (h + 1) * DK_M]
            og = mrow[:, 3 * D_MLSTM + h * DK_M:3 * D_MLSTM + (h + 1) * DK_M]
            li = grow[:, GATE_OFF + h:GATE_OFF + h + 1]
            lf = grow[:, GATE_OFF + H_M + h:GATE_OFF + H_M + h + 1]
            m0 = m_in[:, h:h + 1]
            c_prev = c_ref[b, h]
            n_prev = nb_[h:h + 1, :]
            inter = lf + m0
            m_t = jnp.maximum(inter, li)
            w_intra = jnp.exp(li - m_t)
            w_inter = jnp.exp(inter - m_t)
            s = jnp.sum(q * k, axis=1, keepdims=True) * w_intra
            qc = jnp.sum(to_col(q) * c_prev, axis=0, keepdims=True)
            num = w_inter * qc + s * v
            den = w_inter * jnp.sum(q * n_prev, axis=1, keepdims=True) + s
            hh = num / jnp.maximum(jnp.abs(den), jnp.exp(-m_t))
            hh = hh * jax.nn.sigmoid(og)
            hs.append(_rms(hh, nw_ref[:, h * DK_M:(h + 1) * DK_M]))
            wk = k * w_intra
            c_out[b, h] = w_inter * c_prev + to_col(wk) * v
            n_out[b, h:h + 1, :] = w_inter * n_prev + wk
            ms.append(m_t)
        h_ref[b] = jnp.concatenate(hs, axis=1).astype(BF16)
        m_out[b] = jnp.concatenate(ms, axis=1)


def _mlstm_step(m, g, nw, c, n, mi, nb=8):
    bd = m.shape[0]
    i3 = lambda i: (i, 0, 0)
    outs = [
        jax.ShapeDtypeStruct((bd, 1, D_MLSTM), BF16),
        jax.ShapeDtypeStruct((bd, H_M, DK_M, DK_M), F32),
        jax.ShapeDtypeStruct((bd, H_M, DK_M), F32),
        jax.ShapeDtypeStruct((bd, 1, H_M), F32),
    ]
    return pl.pallas_call(
        _mlstm_step_kernel,
        grid=(bd // nb,),
        in_specs=[
            pl.BlockSpec((nb, 1, 1024), i3), pl.BlockSpec((nb, 1, LANES), i3),
            pl.BlockSpec((1, D_MLSTM), lambda i: (0, 0)),
            pl.BlockSpec((nb, H_M, DK_M, DK_M), lambda i: (i, 0, 0, 0)),
            pl.BlockSpec((nb, H_M, DK_M), i3), pl.BlockSpec((nb, 1, H_M), i3),
        ],
        out_specs=[
            pl.BlockSpec((nb, 1, D_MLSTM), i3),
            pl.BlockSpec((nb, H_M, DK_M, DK_M), lambda i: (i, 0, 0, 0)),
            pl.BlockSpec((nb, H_M, DK_M), i3), pl.BlockSpec((nb, 1, H_M), i3),
        ],
        out_shape=outs,
        compiler_params=_cparams(("parallel",)),
        name="mlstm_step",
    )(m.reshape(bd, 1, 1024), g.reshape(bd, 1, LANES), nw, c, n, mi.reshape(bd, 1, H_M))


def _qproj_kernel(cq_ref, w_ref, c_ref, sa_ref, sb_ref, q_ref, *, scale):
    q = _dot(cq_ref[...], w_ref[...])
    c, sa, sb = c_ref[...], sa_ref[...], sb_ref[...]
    for h in range(H_A):
        sl = slice(h * HEAD_PAD, (h + 1) * HEAD_PAD)
        q_ref[:, sl] = (_rope_lanes(q[:, sl], c, sa, sb) * scale).astype(BF16)


def _qproj(cq, w, tabs, tm, scale):
    t = cq.shape[0]
    n_tab = tabs[0].shape[0] // tm
    row = lambda i: (i, 0)
    tab = lambda i: (i % n_tab, 0)
    return pl.pallas_call(
        functools.partial(_qproj_kernel, scale=scale),
        grid=(t // tm,),
        in_specs=[pl.BlockSpec((tm, Q_LORA), row), pl.BlockSpec((Q_LORA, H_A * HEAD_PAD), lambda i: (0, 0)),
                  pl.BlockSpec((tm, LANES), tab), pl.BlockSpec((tm, LANES), tab), pl.BlockSpec((tm, LANES), tab)],
        out_specs=pl.BlockSpec((tm, H_A * HEAD_PAD), row),
        out_shape=jax.ShapeDtypeStruct((t, H_A * HEAD_PAD), BF16),
        compiler_params=_cparams(("parallel",)),
        name="qproj",
    )(cq, w, *tabs)


def _kvproj_kernel(ckv_ref, kpe_ref, wk_ref, wpe_ref, wvt_ref, vb_ref, k_ref, vt_ref):
    ckv = ckv_ref[...].astype(BF16)
    k_ref[...] = (_dot(ckv, wk_ref[...]) + _dot(kpe_ref[...].astype(BF16), wpe_ref[...])).astype(BF16)
    tk = vt_ref.shape[2]
    for i in range(vt_ref.shape[0]):
        vt_ref[i] = (_dot_nt(wvt_ref[...], ckv[i * tk:(i + 1) * tk, :]) + vb_ref[...]).astype(BF16)


def _kvproj(ckv, kpe, wk, wpe, wvt, vb, tm, tk):
    t = ckv.shape[0]
    row = lambda i: (i, 0)
    const = lambda i: (0, 0)
    n = H_A * HEAD_PAD
    return pl.pallas_call(
        _kvproj_kernel,
        grid=(t // tm,),
        in_specs=[pl.BlockSpec((tm, KV_LORA), row), pl.BlockSpec((tm, ROPE_DIM), row),
                  pl.BlockSpec((KV_LORA, n), const), pl.BlockSpec((ROPE_DIM, n), const),
                  pl.BlockSpec((n, KV_LORA), const), pl.BlockSpec((n, 1), const)],
        out_specs=[pl.BlockSpec((tm, n), row), pl.BlockSpec((tm // tk, n, tk), lambda i: (i, 0, 0))],
        out_shape=[jax.ShapeDtypeStruct((t, n), BF16), jax.ShapeDtypeStruct((t // tk, n, tk), BF16)],
        compiler_params=_cparams(("parallel",)),
        name="kvproj",
    )(ckv, kpe, wk, wpe, wvt, vb)


def _flash_kernel(q_ref, k_ref, vt_ref, o_ref):
    qi = pl.program_id(1)
    tq = q_ref.shape[0]
    tk = vt_ref.shape[2]
    key = lax.broadcasted_iota(jnp.int32, (tk, tq), 0)
    qry = lax.broadcasted_iota(jnp.int32, (tk, tq), 1)
    causal = key <= qry

    def scores(h, j):
        sl = slice(h * HEAD_PAD, (h + 1) * HEAD_PAD)
        off = pl.multiple_of(j * tk, tk)
        return _dot_nt(k_ref[pl.ds(off, tk), sl], q_ref[:, sl])

    def update(h, j, s, carry):
        m, acc = carry
        m_new = jnp.maximum(m, jnp.max(s, axis=0, keepdims=True))
        p = jnp.exp2(s - m_new)
        acc = jnp.exp2(m - m_new) * acc + _dot(vt_ref[j, h * HEAD_PAD:(h + 1) * HEAD_PAD, :], p.astype(BF16))
        return m_new, acc

    for g0 in range(0, H_A, FLASH_HEADS):
        heads = range(g0, g0 + FLASH_HEADS)

        def body(j, state):
            s_cur, carries = state
            j_cur = jnp.where(j == 0, qi, j - 1)
            s_next = tuple(scores(h, j) for h in heads)
            carries = tuple(update(h, j_cur, s, c) for h, s, c in zip(heads, s_cur, carries))
            return s_next, carries

        s_diag = tuple(jnp.where(causal, scores(h, qi), NEG) for h in heads)
        init = tuple((jnp.full((1, tq), NEG, F32), jnp.zeros((HEAD_PAD, tq), F32)) for _ in heads)
        s_last, carries = lax.fori_loop(0, qi, body, (s_diag, init))
        j_last = jnp.maximum(qi - 1, 0)
        carries = tuple(update(h, j_last, s, c) for h, s, c in zip(heads, s_last, carries))
        outs = [(acc / acc[ONES_LANE:ONES_LANE + 1, :]).T[:, :V_A] for _, acc in carries]
        o_ref[:, g0 * V_A:(g0 + FLASH_HEADS) * V_A] = jnp.concatenate(outs, axis=1).astype(BF16)


def _flash(q, k, vt, b, l):
    tq = min(ATTN_TQ, l)
    nq = l // tq
    n = H_A * HEAD_PAD
    return pl.pallas_call(
        _flash_kernel,
        grid=(b, nq),
        in_specs=[pl.BlockSpec((tq, n), lambda i, j: (i * nq + j, 0)),
                  pl.BlockSpec((l, n), lambda i, j: (i, 0)),
                  pl.BlockSpec((nq, n, tq), lambda i, j: (i, 0, 0))],
        out_specs=pl.BlockSpec((tq, D_MLA), lambda i, j: (i * nq + j, 0)),
        out_shape=jax.ShapeDtypeStruct((b * l, D_MLA), BF16),
        compiler_params=_cparams(("parallel", "arbitrary")),
        name="flash",
    )(q, k, vt)


def _matmul_kernel(x_ref, w_ref, o_ref):
    o_ref[...] = _dot(x_ref[...].astype(BF16), w_ref[...]).astype(o_ref.dtype)


def _matmul(x, w, dtype, name):
    t, kd = x.shape
    n = w.shape[1]
    return pl.pallas_call(
        _matmul_kernel,
        grid=(1,),
        in_specs=[pl.BlockSpec((t, kd), lambda i: (0, 0)), pl.BlockSpec((kd, n), lambda i: (0, 0))],
        out_specs=pl.BlockSpec((t, n), lambda i: (0, 0)),
        out_shape=jax.ShapeDtypeStruct((t, n), dtype),
        compiler_params=_cparams(("arbitrary",)),
        name=name,
    )(x, w)


def _decode_kernel(pt_ref, ql_ref, qp_ref, cn_ref, kn_ref, ckv_hbm, kpet_hbm, o_ref,
                   cbuf, kbuf, csem, ksem, *, layer, n_chunks, g_pages, n_seq):
    step = pl.program_id(0)
    n_steps = pl.num_programs(0)

    def copies(st, c, slot):
        out = []
        for e in range(n_seq):
            for g in range(g_pages):
                page = pt_ref[st * n_seq + e, c * g_pages + g]
                dst = pl.ds(g * PAGE_SIZE, PAGE_SIZE)
                out.append(pltpu.make_async_copy(ckv_hbm.at[layer, page], cbuf.at[slot, e, dst], csem.at[slot]))
                out.append(pltpu.make_async_copy(kpet_hbm.at[layer, page], kbuf.at[slot, e, :, dst], ksem.at[slot]))
        return out

    def start(st, c, slot):
        for cp in copies(st, c, slot):
            cp.start()

    def wait(st, c, slot):
        for cp in copies(st, c, slot):
            cp.wait()

    @pl.when(step == 0)
    def _():
        start(0, 0, 0)

    qls = [ql_ref[e].astype(BF16) for e in range(n_seq)]
    qps = [qp_ref[e].astype(BF16) for e in range(n_seq)]

    def chunk(c, carries):
        flat = step * n_chunks + c
        slot = flat % 2
        nxt_c = jnp.where(c + 1 < n_chunks, c + 1, 0)
        nxt_s = jnp.where(c + 1 < n_chunks, step, step + 1)

        @pl.when(nxt_s < n_steps)
        def _():
            start(nxt_s, nxt_c, 1 - slot)

        wait(step, c, slot)
        out = []
        for e, (m, l, acc) in enumerate(carries):
            ckv = cbuf[slot, e].astype(BF16)
            kpet = kbuf[slot, e].astype(BF16)
            s = (_dot_nt(qls[e], ckv) + _dot(qps[e], kpet)) * ATTN_SCALE
            m_new = jnp.maximum(m, jnp.max(s, axis=1, keepdims=True))
            p = jnp.exp(s - m_new)
            alpha = jnp.exp(m - m_new)
            l = alpha * l + jnp.sum(p, axis=1, keepdims=True)
            acc = alpha * acc + _dot(p.astype(BF16), ckv)
            out.append((m_new, l, acc))
        return tuple(out)

    init = tuple((jnp.full((H_A, 1), NEG, F32), jnp.zeros((H_A, 1), F32), jnp.zeros((H_A, KV_LORA), F32))
                 for _ in range(n_seq))
    carries = lax.fori_loop(0, n_chunks, chunk, init)

    for e, (m, l, acc) in enumerate(carries):
        cn = cn_ref[e].astype(BF16).astype(F32)
        kn = kn_ref[e].astype(BF16).astype(F32)
        s_new = (jnp.sum(qls[e].astype(F32) * cn, axis=1, keepdims=True)
                 + jnp.sum(qps[e].astype(F32) * kn, axis=1, keepdims=True)) * ATTN_SCALE
        m_f = jnp.maximum(m, s_new)
        p_new = jnp.exp(s_new - m_f)
        alpha = jnp.exp(m - m_f)
        l = alpha * l + p_new
        acc = alpha * acc + p_new.astype(BF16).astype(F32) * cn
        o_ref[e] = acc / l


def _decode(page_table, ql, qp, cn, kn, cache_ckv, cache_kpet, layer):
    bd, n_pages = page_table.shape
    g_pages = min(DECODE_PAGES, n_pages)
    n_chunks = n_pages // g_pages
    n_seq = min(DECODE_SEQS, bd)
    rows = g_pages * PAGE_SIZE
    i3 = lambda i, pt: (i, 0, 0)
    grid_spec = pltpu.PrefetchScalarGridSpec(
        num_scalar_prefetch=1,
        grid=(bd // n_seq,),
        in_specs=[pl.BlockSpec((n_seq, H_A, KV_LORA), i3), pl.BlockSpec((n_seq, H_A, ROPE_DIM), i3),
                  pl.BlockSpec((n_seq, 1, KV_LORA), i3), pl.BlockSpec((n_seq, 1, ROPE_DIM), i3),
                  pl.BlockSpec(memory_space=pl.ANY), pl.BlockSpec(memory_space=pl.ANY)],
        out_specs=pl.BlockSpec((n_seq, H_A, KV_LORA), i3),
        scratch_shapes=[pltpu.VMEM((2, n_seq, rows, KV_LORA), F32), pltpu.VMEM((2, n_seq, ROPE_DIM, rows), F32),
                        pltpu.SemaphoreType.DMA((2,)), pltpu.SemaphoreType.DMA((2,))],
    )
    return pl.pallas_call(
        functools.partial(_decode_kernel, layer=layer, n_chunks=n_chunks, g_pages=g_pages, n_seq=n_seq),
        grid_spec=grid_spec,
        out_shape=jax.ShapeDtypeStruct((bd, H_A, KV_LORA), F32),
        compiler_params=_cparams(("arbitrary",)),
        name="decode",
    )(page_table, ql, qp, cn, kn, cache_ckv, cache_kpet)


def _pool_kernel(u_ref, w_ref, sc_ref, y_ref):
    u = u_ref[0]
    row = lax.broadcasted_iota(jnp.int32, u.shape, 0)
    lane = lax.broadcasted_iota(jnp.int32, u.shape, 1)
    cnt = (row + 1).astype(F32)
    acc = u
    mean = jnp.zeros_like(u)
    k = 1
    for gi, w in enumerate(POOL_WINDOWS):
        while k < w:
            acc = acc + _shift_rows(acc, k, row)
            k *= 2
        in_group = (lane >= gi * D_POOL_G) & (lane < (gi + 1) * D_POOL_G)
        mean = jnp.where(in_group, acc / jnp.minimum(cnt, float(w)), mean)
    d = (mean - u).astype(BF16)
    y_ref[0] = (_dot(d, w_ref[...]) * sc_ref[...]).astype(BF16)


def _pool(u, w, sc):
    b, l, _ = u.shape
    i3 = lambda i: (i, 0, 0)
    return pl.pallas_call(
        _pool_kernel,
        grid=(b,),
        in_specs=[pl.BlockSpec((1, l, D_POOL), i3), pl.BlockSpec((D_POOL, D_POOL), lambda i: (0, 0)),
                  pl.BlockSpec((1, D_POOL), lambda i: (0, 0))],
        out_specs=pl.BlockSpec((1, l, D_POOL), i3),
        out_shape=jax.ShapeDtypeStruct((b, l, D_POOL), BF16),
        compiler_params=_cparams(("parallel",)),
        name="pool",
    )(u, w, sc)


def _outproj_kernel(x_ref, hm_ref, oa_ref, yp_ref, z_ref, w_ref, pw_ref, o_ref):
    z = z_ref[...].astype(F32)
    g = _silu(z)
    ym = (hm_ref[...].astype(F32) * g[:, :D_MLSTM]).astype(BF16)
    ya = (oa_ref[...].astype(F32) * g[:, D_MLSTM:D_MLSTM + D_MLA]).astype(BF16)
    yp = (yp_ref[...].astype(F32) * g[:, D_MLSTM + D_MLA:]).astype(BF16)
    mix = (_dot(ym, w_ref[:D_MLSTM, :]) + _dot(ya, w_ref[D_MLSTM:D_MLSTM + D_MLA, :])
           + _dot(yp, w_ref[D_MLSTM + D_MLA:, :]))
    o_ref[...] = x_ref[...] + _rms(mix, pw_ref[...])


def _outproj(x, hm, oa, yp, z, w, pw, tm):
    t = x.shape[0]
    row = lambda i: (i, 0)
    const = lambda i: (0, 0)
    return pl.pallas_call(
        _outproj_kernel,
        grid=(t // tm,),
        in_specs=[pl.BlockSpec((tm, D_MODEL), row), pl.BlockSpec((tm, D_MLSTM), row),
                  pl.BlockSpec((tm, D_MLA), row), pl.BlockSpec((tm, D_POOL), row),
                  pl.BlockSpec((tm, D_MODEL), row), pl.BlockSpec((D_MODEL, D_MODEL), const),
                  pl.BlockSpec((1, D_MODEL), const)],
        out_specs=pl.BlockSpec((tm, D_MODEL), row),
        out_shape=jax.ShapeDtypeStruct((t, D_MODEL), F32),
        compiler_params=_cparams(("parallel",)),
        name="outproj",
    )(x, hm, oa, yp, z, w, pw)


def _rope_tables(pos, off):
    inv = ROPE_THETA ** (-jnp.arange(ROPE_HALF, dtype=F32) / ROPE_HALF)
    ang = pos.astype(F32)[:, None] * inv[None, :]
    cos, sin = jnp.cos(ang), jnp.sin(ang)
    n = pos.shape[0]
    z16 = jnp.zeros((n, ROPE_HALF), F32)
    pad_l = lambda fill: jnp.full((n, off), fill, F32)
    pad_r = jnp.zeros((n, LANES - off - ROPE_DIM), F32)
    c = jnp.concatenate([pad_l(1.0), cos, cos, pad_r], axis=1)
    sa = jnp.concatenate([pad_l(0.0), -sin, z16, pad_r], axis=1)
    sb = jnp.concatenate([pad_l(0.0), z16, sin, pad_r], axis=1)
    return c, sa, sb


def _layer_weights(w_in, gate_b, w_uq, w_uk, w_uv, pool_w, w_out):
    o = np.cumsum([0, 256, 256, 256, 4, 4, 256, 256, 384, 256, 32, 512, 256, 256])
    col = lambda i: w_in[:, o[i]:o[i + 1]]
    q_m, k_m, v_m, ig, fg, og, z_m, cq, ckv, kpe, z_a, u_p, z_p = [col(i) for i in range(13)]
    zc = lambda n: jnp.zeros((D_MODEL, n), F32)
    w_all = jnp.concatenate([
        q_m, k_m * (DK_M ** -0.5), v_m, og, z_m, z_a, z_p, cq, ckv, u_p,
        kpe, zc(GATE_OFF - ROPE_DIM), ig, fg, zc(LANES - GATE_OFF - 2 * H_M)], axis=1).astype(BF16)
    gb = jnp.concatenate([jnp.zeros((GATE_OFF,), F32), gate_b, jnp.zeros((LANES - GATE_OFF - 2 * H_M,), F32)])[None]

    pad_q = jnp.zeros((Q_LORA, H_A, HEAD_PAD - NOPE - ROPE_DIM), F32)
    wq = jnp.concatenate([w_uq, pad_q], axis=2).reshape(Q_LORA, H_A * HEAD_PAD).astype(BF16)
    wk = jnp.concatenate([w_uk, jnp.zeros((KV_LORA, H_A, HEAD_PAD - NOPE), F32)], axis=2)
    wk = wk.reshape(KV_LORA, H_A * HEAD_PAD).astype(BF16)
    place = jnp.concatenate([jnp.zeros((ROPE_DIM, PE_OFF), F32), jnp.eye(ROPE_DIM, dtype=F32),
                             jnp.zeros((ROPE_DIM, HEAD_PAD - PE_OFF - ROPE_DIM), F32)], axis=1)
    wpe = jnp.tile(place, (1, H_A)).astype(BF16)
    wv = jnp.concatenate([w_uv, jnp.zeros((KV_LORA, H_A, HEAD_PAD - V_A), F32)], axis=2)
    wvt = wv.reshape(KV_LORA, H_A * HEAD_PAD).T.astype(BF16)
    vb = jnp.tile((jnp.arange(HEAD_PAD) == ONES_LANE).astype(F32), H_A)[:, None]

    eye_h = jnp.eye(H_A, dtype=F32)
    wabs = jnp.concatenate([jnp.transpose(w_uk, (1, 2, 0)),
                            jnp.zeros((H_A, HEAD_PAD - NOPE, KV_LORA), F32)], axis=1)
    wabs = jnp.einsum('hdc,hg->hdgc', wabs, eye_h).reshape(H_A * HEAD_PAD, H_A * KV_LORA).astype(BF16)
    wuv_bd = jnp.einsum('chd,hg->hcgd', w_uv, eye_h).reshape(H_A * KV_LORA, D_MLA).astype(BF16)
    eye_g = jnp.eye(len(POOL_WINDOWS), dtype=F32)
    wpool = jnp.einsum('gcd,gk->gckd', pool_w, eye_g).reshape(D_POOL, D_POOL).astype(BF16)
    return dict(w_all=w_all, gb=gb, wq=wq, wk=wk, wpe=wpe, wvt=wvt, vb=vb, wabs=wabs, wuv_bd=wuv_bd,
                wpool=wpool, w_out=w_out.astype(BF16))


def _row_tile(t):
    return min(512, t)


def kernel(x_prompt, x_sample, state_mlstm_C, state_mlstm_n, state_mlstm_m, cache_ckv, cache_kpe, state_pool, page_table, norm_pre_w, norm_post_w, w_in, mlstm_gate_b, mlstm_norm_w, mla_q_norm_w, mla_kv_norm_w, mla_w_uq, mla_w_uk, mla_w_uv, pool_w, pool_scale, w_out):
    b, l, _ = x_prompt.shape
    bd = x_sample.shape[0]
    depth = w_in.shape[0]
    n_past = page_table.shape[1] * PAGE_SIZE
    tp = b * l
    tm_p, tm_s = _row_tile(min(tp, l)), _row_tile(bd)

    pos_p = jnp.arange(l)
    pos_s = jnp.full((tm_s,), n_past)
    tabs_in_p, tabs_q_p = _rope_tables(pos_p, 0), _rope_tables(pos_p, PE_OFF)
    tabs_in_s, tabs_q_s = _rope_tables(pos_s, 0), _rope_tables(pos_s, PE_OFF)

    cache_kpet = jnp.swapaxes(cache_kpe, 2, 3)
    yp = x_prompt.reshape(tp, D_MODEL)
    ys = x_sample.reshape(bd, D_MODEL)
    outs = [[] for _ in range(12)]
    for li in range(depth):
        lw = _layer_weights(w_in[li], mlstm_gate_b[li], mla_w_uq[li], mla_w_uk[li], mla_w_uv[li],
                            pool_w[li], w_out[li])
        prew, postw = norm_pre_w[li][None], norm_post_w[li][None]
        qnw, kvnw, mnw = mla_q_norm_w[li][None], mla_kv_norm_w[li][None], mlstm_norm_w[li][None]
        psc = pool_scale[li][None]

        m, z, cq, ckv, u, kpe, g = _inproj(yp, prew, lw['w_all'], qnw, kvnw, lw['gb'], tabs_in_p, tm_p)
        hm, c_f, n_f, m_f = _mlstm_prompt(m, g, mnw, b, l)
        q = _qproj(cq, lw['wq'], tabs_q_p, tm_p, ATTN_SCALE * LOG2E)
        kk, vt = _kvproj(ckv, kpe, lw['wk'], lw['wpe'], lw['wvt'], lw['vb'], tm_p, min(ATTN_TQ, l))
        oa = _flash(q, kk, vt, b, l)
        u3 = u.reshape(b, l, D_POOL)
        ypool = _pool(u3, lw['wpool'], psc).reshape(tp, D_POOL)
        yp = _outproj(yp, hm, oa, ypool, z, lw['w_out'], postw, tm_p)
        outs[0].append(c_f)
        outs[1].append(n_f)
        outs[2].append(m_f[:, :, 0])
        outs[3].append(ckv.reshape(b, l, KV_LORA))
        outs[4].append(kpe.reshape(b, l, ROPE_DIM))
        outs[5].append(u3[:, l - POOL_BUF:])

        m, z, cq, ckv, u, kpe, g = _inproj(ys, prew, lw['w_all'], qnw, kvnw, lw['gb'], tabs_in_s, tm_s)
        hm, c_n, n_n, m_n = _mlstm_step(m, g, mnw, state_mlstm_C[li], state_mlstm_n[li], state_mlstm_m[li])
        q = _qproj(cq, lw['wq'], tabs_q_s, tm_s, 1.0)
        ql = _matmul(q, lw['wabs'], F32, "absorb").reshape(bd, H_A, KV_LORA)
        qp = q.reshape(bd, H_A, HEAD_PAD)[:, :, PE_OFF:PE_OFF + ROPE_DIM]
        o_lat = _decode(page_table, ql, qp, ckv.reshape(bd, 1, KV_LORA), kpe.reshape(bd, 1, ROPE_DIM),
                        cache_ckv, cache_kpet, li)
        oa = _matmul(o_lat.reshape(bd, H_A * KV_LORA), lw['wuv_bd'], BF16, "uv")
        ext = jnp.concatenate([state_pool[li], u[:, None, :]], axis=1)
        ypool = _pool(ext, lw['wpool'], psc)[:, POOL_BUF]
        ys = _outproj(ys, hm.reshape(bd, D_MLSTM), oa, ypool, z, lw['w_out'], postw, tm_s)
        outs[6].append(c_n)
        outs[7].append(n_n)
        outs[8].append(m_n.reshape(bd, H_M))
        outs[9].append(ckv.reshape(bd, 1, KV_LORA))
        outs[10].append(kpe.reshape(bd, 1, ROPE_DIM))
        outs[11].append(ext[:, 1:])

    return (yp.reshape(b, l, D_MODEL), ys.reshape(bd, 1, D_MODEL)) + tuple(jnp.stack(o) for o in outs)
```

```python
import functools

import jax
import jax.numpy as jnp
import numpy as np
from jax import lax
from jax.experimental import pallas as pl
from jax.experimental.pallas import tpu as pltpu

F32 = jnp.float32
BF16 = jnp.bfloat16

D_MODEL = 1024
H_M = 4
D_MLSTM = 256
DK_M = 64
H_A = 8
D_MLA = 512
V_A = 64
NOPE = 64
ROPE_DIM = 32
ROPE_HALF = ROPE_DIM // 2
Q_LORA = 384
KV_LORA = 256
ROPE_THETA = 10000.0
ATTN_SCALE = (NOPE + ROPE_DIM) ** -0.5
D_POOL = 256
D_POOL_G = 64
POOL_WINDOWS = (2, 4, 8, 16)
POOL_BUF = 15
PAGE_SIZE = 128
RMS_EPS = 1e-6
NEG = -1e30

LANES = 128
HEAD_PAD = LANES
PE_OFF = NOPE
ONES_LANE = V_A
VMEM_LIMIT = 56 * 1024 * 1024

MLSTM_CHUNK = 256
ATTN_TQ = 256
FLASH_HEADS = 8
LOG2E = 1.4426950408889634
DECODE_PAGES = 8
DECODE_SEQS = 4
DECODE_SLOTS = 3


def _cparams(sem):
    return pltpu.CompilerParams(dimension_semantics=sem, vmem_limit_bytes=VMEM_LIMIT)


def _rms(x, w):
    return x * lax.rsqrt(jnp.mean(x * x, axis=-1, keepdims=True) + RMS_EPS) * w


def _silu(z):
    return z * jax.nn.sigmoid(z)


def _rope_lanes(x, c, s):
    return x * c + pltpu.roll(x, LANES - ROPE_HALF, 1) * s


def _shift_rows(x, k, row):
    return jnp.where(row >= k, pltpu.roll(x, k, 0), 0.0)


def _dot(a, b):
    return jnp.dot(a, b, preferred_element_type=F32)


def _dot_nt(a, b):
    return lax.dot_general(a, b, (((1,), (1,)), ((), ())), preferred_element_type=F32)


def _dot_tn(a, b):
    return lax.dot_general(a, b, (((0,), (0,)), ((), ())), preferred_element_type=F32)


C_M = 0
C_Z = 1024
C_CQ = 2048
C_CKV = 2432
C_U = 2688
C_X = 2944
N_IN = 3072
GATE_OFF = 64


def _inproj_kernel(x_ref, prew_ref, w_ref, qnw_ref, kvnw_ref, gb_ref, c_ref, s_ref,
                   m_ref, z_ref, cq_ref, ckv_ref, u_ref, kpe_ref, g_ref):
    x = x_ref[...]
    h = _rms(x, prew_ref[...]).astype(BF16)
    m_ref[...] = _dot(h, w_ref[:, C_M:C_Z]).astype(BF16)
    z_ref[...] = _dot(h, w_ref[:, C_Z:C_CQ]).astype(BF16)
    cq_ref[...] = _rms(_dot(h, w_ref[:, C_CQ:C_CKV]), qnw_ref[...]).astype(BF16)
    ckv_ref[...] = _rms(_dot(h, w_ref[:, C_CKV:C_U]), kvnw_ref[...])
    u_ref[...] = _dot(h, w_ref[:, C_U:C_X])
    xg = _dot(h, w_ref[:, C_X:N_IN])
    kpe_ref[...] = _rope_lanes(xg, c_ref[...], s_ref[...])[:, :ROPE_DIM]
    pre = xg + gb_ref[...]
    lane = lax.broadcasted_iota(jnp.int32, pre.shape, 1)
    logsig = jnp.minimum(pre, 0.0) - jnp.log1p(jnp.exp(-jnp.abs(pre)))
    g_ref[...] = jnp.where(lane < GATE_OFF + H_M, pre, logsig)


def _inproj(x, prew, w, qnw, kvnw, gb, tabs, tm):
    t = x.shape[0]
    n_tab = tabs[0].shape[0] // tm
    row = lambda i: (i, 0)
    const = lambda i: (0, 0)
    tab = lambda i: (i % n_tab, 0)
    outs = [
        jax.ShapeDtypeStruct((t, 1024), BF16), jax.ShapeDtypeStruct((t, 1024), BF16),
        jax.ShapeDtypeStruct((t, Q_LORA), BF16), jax.ShapeDtypeStruct((t, KV_LORA), F32),
        jax.ShapeDtypeStruct((t, D_POOL), F32), jax.ShapeDtypeStruct((t, ROPE_DIM), F32),
        jax.ShapeDtypeStruct((t, LANES), F32),
    ]
    return pl.pallas_call(
        _inproj_kernel,
        grid=(t // tm,),
        in_specs=[
            pl.BlockSpec((tm, D_MODEL), row), pl.BlockSpec((1, D_MODEL), const),
            pl.BlockSpec((D_MODEL, N_IN), const), pl.BlockSpec((1, Q_LORA), const),
            pl.BlockSpec((1, KV_LORA), const), pl.BlockSpec((1, LANES), const),
            pl.BlockSpec((tm, LANES), tab), pl.BlockSpec((tm, LANES), tab),
        ],
        out_specs=[pl.BlockSpec((tm, o.shape[1]), row) for o in outs],
        out_shape=outs,
        compiler_params=_cparams(("parallel",)),
        name="inproj",
    )(x, prew, w, qnw, kvnw, gb, *tabs)


def _mlstm_kernel(m_ref, g_ref, nw_ref, h_ref, c_out, n_out, m_out, ct_s, m_s):
    ci = pl.program_id(1)
    lc = m_ref.shape[0]

    @pl.when(ci == 0)
    def _():
        ct_s[...] = jnp.zeros_like(ct_s)
        m_s[...] = jnp.zeros_like(m_s)

    g = g_ref[...]
    lf = pltpu.roll(g, LANES - H_M, 1)
    tril = jnp.where(lax.broadcasted_iota(jnp.int32, (lc, lc), 0) >= lax.broadcasted_iota(jnp.int32, (lc, lc), 1),
                     1.0, 0.0).astype(BF16)
    hi = lf.astype(BF16)
    r1 = lf - hi.astype(F32)
    mid = r1.astype(BF16)
    lo = (r1 - mid.astype(F32)).astype(BF16)
    bcum = _dot(tril, hi) + _dot(tril, mid) + _dot(tril, lo)
    a = g - bcum
    gt = g.T
    bt = bcum.T
    causal = lax.broadcasted_iota(jnp.int32, (lc, lc), 0) <= lax.broadcasted_iota(jnp.int32, (lc, lc), 1)
    lane = lax.broadcasted_iota(jnp.int32, (lc, LANES), 1)
    low_rows = lax.broadcasted_iota(jnp.int32, (LANES, lc), 0) < DK_M

    for p in range(H_M // 2):
        qp = m_ref[:, p * LANES:(p + 1) * LANES]
        kp = m_ref[:, D_MLSTM + p * LANES:D_MLSTM + (p + 1) * LANES]
        vp = m_ref[:, 2 * D_MLSTM + p * LANES:2 * D_MLSTM + (p + 1) * LANES]
        ogt = m_ref[:, 3 * D_MLSTM + p * LANES:3 * D_MLSTM + (p + 1) * LANES].astype(F32).T
        halves = []
        for e in range(2):
            h = 2 * p + e
            ones_at = DK_M * (1 - e)
            own = jnp.where((lane >= DK_M * e) & (lane < DK_M * (e + 1)), 1.0, 0.0).astype(BF16)
            k_h = kp * own
            v_ext = vp * own + jnp.where(lane == ones_at, 1.0, 0.0).astype(BF16)
            r = GATE_OFF + h
            li_r, b_r, a_c = gt[r:r + 1, :], bt[r:r + 1, :], a[:, r:r + 1]
            m_prev = m_s[h:h + 1, 0:1]
            ct_prev = ct_s[h]

            d = jnp.where(causal, b_r + a_c, NEG)
            inter = b_r + m_prev
            m_t = jnp.maximum(inter, jnp.max(d, axis=0, keepdims=True))
            st = _dot_nt(k_h, qp) * jnp.exp(d - m_t)
            tot = jnp.exp(inter - m_t) * _dot_nt(ct_prev.astype(BF16), qp) + _dot_tn(v_ext, st.astype(BF16))
            den = tot[ones_at:ones_at + 1, :]
            halves.append(tot * (1.0 / jnp.maximum(jnp.abs(den), jnp.exp(-m_t))))

            b_last = b_r[:, lc - 1:lc]
            m_new = jnp.maximum(b_last + m_prev, jnp.max(b_last + li_r - b_r, axis=1, keepdims=True))
            wk = (k_h.astype(F32) * jnp.exp(b_last + a_c - m_new)).astype(BF16)
            ct_s[h] = jnp.exp(b_last + m_prev - m_new) * ct_prev + _dot_tn(v_ext, wk)
            m_s[h:h + 1, :] = jnp.broadcast_to(m_new, (1, LANES))

        hh = jnp.where(low_rows, halves[0], halves[1]) * jax.nn.sigmoid(ogt)
        sq = hh * hh
        inv = jnp.where(low_rows,
                        lax.rsqrt(jnp.mean(sq[:DK_M], axis=0, keepdims=True) + RMS_EPS),
                        lax.rsqrt(jnp.mean(sq[DK_M:], axis=0, keepdims=True) + RMS_EPS))
        y = hh * inv * nw_ref[p * LANES:(p + 1) * LANES, :]
        h_ref[:, p * LANES:(p + 1) * LANES] = y.T.astype(BF16)

    @pl.when(ci == pl.num_programs(1) - 1)
    def _():
        for h in range(H_M):
            e = h % 2
            sl = slice(DK_M * e, DK_M * (e + 1))
            ones_at = DK_M * (1 - e)
            c_out[0, h] = ct_s[h].T[sl, sl]
            n_out[0, h:h + 1, :] = ct_s[h][ones_at:ones_at + 1, sl]
        m_out[0] = m_s[0:H_M, :]


def _mlstm_prompt(m, g, nw, b, l):
    lc = min(MLSTM_CHUNK, l)
    nc = l // lc
    row = lambda i, j: (i * nc + j, 0)
    outs = [
        jax.ShapeDtypeStruct((b * l, D_MLSTM), BF16),
        jax.ShapeDtypeStruct((b, H_M, DK_M, DK_M), F32),
        jax.ShapeDtypeStruct((b, H_M, DK_M), F32),
        jax.ShapeDtypeStruct((b, H_M, LANES), F32),
    ]
    nw_rep = jnp.broadcast_to(nw.reshape(D_MLSTM, 1), (D_MLSTM, lc))
    return pl.pallas_call(
        _mlstm_kernel,
        grid=(b, nc),
        in_specs=[
            pl.BlockSpec((lc, 1024), row), pl.BlockSpec((lc, LANES), row),
            pl.BlockSpec((D_MLSTM, lc), lambda i, j: (0, 0)),
        ],
        out_specs=[
            pl.BlockSpec((lc, D_MLSTM), row),
            pl.BlockSpec((1, H_M, DK_M, DK_M), lambda i, j: (i, 0, 0, 0)),
            pl.BlockSpec((1, H_M, DK_M), lambda i, j: (i, 0, 0)),
            pl.BlockSpec((1, H_M, LANES), lambda i, j: (i, 0, 0)),
        ],
        out_shape=outs,
        scratch_shapes=[pltpu.VMEM((H_M, LANES, LANES), F32), pltpu.VMEM((8, LANES), F32)],
        compiler_params=_cparams(("parallel", "arbitrary")),
        name="mlstm_prompt",
    )(m, g, nw_rep)


def _mlstm_stepT_kernel(q_ref, k_ref, v_ref, og_ref, n_ref, gm_ref, nw_ref, c_ref,
                        h_ref, c_out, n_out, m_out):
    to_t = lambda ref: ref[...].astype(F32).T
    qt, kt, vt, ogt, nt = to_t(q_ref), to_t(k_ref), to_t(v_ref), to_t(og_ref), to_t(n_ref)
    gm = gm_ref[0]
    ys, ns = [], []
    for e in range(2):
        sl = slice(DK_M * e, DK_M * (e + 1))
        q, k, v, n_prev = qt[sl], kt[sl], vt[sl], nt[sl]
        li, lf, m0 = gm[e:e + 1], gm[2 + e:3 + e], gm[4 + e:5 + e]
        inter = lf + m0
        m_t = jnp.maximum(inter, li)
        w_intra = jnp.exp(li - m_t)
        w_inter = jnp.exp(inter - m_t)
        s = jnp.sum(q * k, axis=0, keepdims=True) * w_intra
        wk = k * w_intra
        qc = jnp.zeros_like(v)
        for d in range(DK_M):
            c_d = c_ref[e, d]
            qc = qc + q[d:d + 1] * c_d
            c_out[e, d] = w_inter * c_d + wk[d:d + 1] * v
        num = w_inter * qc + s * v
        den = w_inter * jnp.sum(q * n_prev, axis=0, keepdims=True) + s
        hh = num / jnp.maximum(jnp.abs(den), jnp.exp(-m_t)) * jax.nn.sigmoid(ogt[sl])
        ys.append(hh * lax.rsqrt(jnp.mean(hh * hh, axis=0, keepdims=True) + RMS_EPS))
        ns.append(w_inter * n_prev + wk)
        m_out[0, e:e + 1, :] = m_t
    h_ref[...] = (jnp.concatenate(ys, axis=0) * nw_ref[...]).T.astype(BF16)
    n_out[...] = jnp.concatenate(ns, axis=0).T


def _mlstm_stepT(m, g, nw, c, n, mi):
    bd = m.shape[0]
    n_pair = H_M // 2
    ct = jnp.transpose(c, (1, 2, 3, 0))
    gt = g[:, GATE_OFF:GATE_OFF + 2 * H_M].T
    rows = [jnp.stack([gt[2 * p], gt[2 * p + 1], gt[H_M + 2 * p], gt[H_M + 2 * p + 1],
                       mi[:, 2 * p], mi[:, 2 * p + 1], jnp.zeros((bd,), F32), jnp.zeros((bd,), F32)])
            for p in range(n_pair)]
    gm = jnp.stack(rows)
    nw_rep = jnp.broadcast_to(nw.reshape(D_MLSTM, 1), (D_MLSTM, bd))
    col = lambda off: (lambda p: (0, off + p))
    outs = [
        jax.ShapeDtypeStruct((bd, D_MLSTM), BF16),
        jax.ShapeDtypeStruct((H_M, DK_M, DK_M, bd), F32),
        jax.ShapeDtypeStruct((bd, D_MLSTM), F32),
        jax.ShapeDtypeStruct((n_pair, 2, bd), F32),
    ]
    h, ct_new, n_new, m_new = pl.pallas_call(
        _mlstm_stepT_kernel,
        grid=(n_pair,),
        in_specs=[
            pl.BlockSpec((bd, LANES), col(0)), pl.BlockSpec((bd, LANES), col(2)),
            pl.BlockSpec((bd, LANES), col(4)), pl.BlockSpec((bd, LANES), col(6)),
            pl.BlockSpec((bd, LANES), col(0)), pl.BlockSpec((1, 8, bd), lambda p: (p, 0, 0)),
            pl.BlockSpec((LANES, bd), lambda p: (p, 0)),
            pl.BlockSpec((2, DK_M, DK_M, bd), lambda p: (p, 0, 0, 0)),
        ],
        out_specs=[
            pl.BlockSpec((bd, LANES), col(0)),
            pl.BlockSpec((2, DK_M, DK_M, bd), lambda p: (p, 0, 0, 0)),
            pl.BlockSpec((bd, LANES), col(0)),
            pl.BlockSpec((1, 2, bd), lambda p: (p, 0, 0)),
        ],
        out_shape=outs,
        compiler_params=_cparams(("parallel",)),
        name="mlstm_step",
    )(m, m, m, m, n.reshape(bd, D_MLSTM), gm, nw_rep, ct)
    return (h, jnp.transpose(ct_new, (3, 0, 1, 2)), n_new.reshape(bd, H_M, DK_M),
            m_new.reshape(H_M, bd).T)


def _qproj_kernel(cq_ref, w_ref, c_ref, s_ref, q_ref):
    q = _dot(cq_ref[...], w_ref[...])
    c, s = c_ref[...], s_ref[...]
    for h in range(H_A):
        sl = slice(h * HEAD_PAD, (h + 1) * HEAD_PAD)
        q_ref[:, sl] = _rope_lanes(q[:, sl], c, s).astype(BF16)


def _qproj(cq, w, tabs, tm):
    t = cq.shape[0]
    n_tab = tabs[0].shape[0] // tm
    row = lambda i: (i, 0)
    tab = lambda i: (i % n_tab, 0)
    return pl.pallas_call(
        _qproj_kernel,
        grid=(t // tm,),
        in_specs=[pl.BlockSpec((tm, Q_LORA), row), pl.BlockSpec((Q_LORA, H_A * HEAD_PAD), lambda i: (0, 0)),
                  pl.BlockSpec((tm, LANES), tab), pl.BlockSpec((tm, LANES), tab)],
        out_specs=pl.BlockSpec((tm, H_A * HEAD_PAD), row),
        out_shape=jax.ShapeDtypeStruct((t, H_A * HEAD_PAD), BF16),
        compiler_params=_cparams(("parallel",)),
        name="qproj",
    )(cq, w, *tabs)


def _kvproj_kernel(ckv_ref, kpe_ref, wk_ref, wpe_ref, wvt_ref, vb_ref, k_ref, vt_ref):
    ckv = ckv_ref[...].astype(BF16)
    k_ref[...] = (_dot(ckv, wk_ref[...]) + _dot(kpe_ref[...].astype(BF16), wpe_ref[...])).astype(BF16)
    tk = vt_ref.shape[2]
    for i in range(vt_ref.shape[0]):
        vt_ref[i] = (_dot_nt(wvt_ref[...], ckv[i * tk:(i + 1) * tk, :]) + vb_ref[...]).astype(BF16)


def _kvproj(ckv, kpe, wk, wpe, wvt, vb, tm, tk):
    t = ckv.shape[0]
    row = lambda i: (i, 0)
    const = lambda i: (0, 0)
    n = H_A * HEAD_PAD
    return pl.pallas_call(
        _kvproj_kernel,
        grid=(t // tm,),
        in_specs=[pl.BlockSpec((tm, KV_LORA), row), pl.BlockSpec((tm, ROPE_DIM), row),
                  pl.BlockSpec((KV_LORA, n), const), pl.BlockSpec((ROPE_DIM, n), const),
                  pl.BlockSpec((n, KV_LORA), const), pl.BlockSpec((n, 1), const)],
        out_specs=[pl.BlockSpec((tm, n), row), pl.BlockSpec((tm // tk, n, tk), lambda i: (i, 0, 0))],
        out_shape=[jax.ShapeDtypeStruct((t, n), BF16), jax.ShapeDtypeStruct((t // tk, n, tk), BF16)],
        compiler_params=_cparams(("parallel",)),
        name="kvproj",
    )(ckv, kpe, wk, wpe, wvt, vb)


def _flash_kernel(q_ref, k_ref, vt_ref, o_ref):
    qi = pl.program_id(1)
    tq = q_ref.shape[0]
    tk = vt_ref.shape[2]
    key = lax.broadcasted_iota(jnp.int32, (tk, tq), 0)
    qry = lax.broadcasted_iota(jnp.int32, (tk, tq), 1)
    causal = key <= qry

    def scores(h, j):
        sl = slice(h * HEAD_PAD, (h + 1) * HEAD_PAD)
        off = pl.multiple_of(j * tk, tk)
        return _dot_nt(k_ref[pl.ds(off, tk), sl], q_ref[:, sl])

    def update(h, j, s, carry):
        m, acc = carry
        m_new = jnp.maximum(m, jnp.max(s, axis=0, keepdims=True))
        p = jnp.exp2(s - m_new)
        acc = jnp.exp2(m - m_new) * acc + _dot(vt_ref[j, h * HEAD_PAD:(h + 1) * HEAD_PAD, :], p.astype(BF16))
        return m_new, acc

    for g0 in range(0, H_A, FLASH_HEADS):
        heads = range(g0, g0 + FLASH_HEADS)

        def body(j, state):
            s_cur, carries = state
            j_cur = jnp.where(j == 0, qi, j - 1)
            s_next = tuple(scores(h, j) for h in heads)
            carries = tuple(update(h, j_cur, s, c) for h, s, c in zip(heads, s_cur, carries))
            return s_next, carries

        s_diag = tuple(jnp.where(causal, scores(h, qi), NEG) for h in heads)
        init = tuple((jnp.full((1, tq), NEG, F32), jnp.zeros((HEAD_PAD, tq), F32)) for _ in heads)
        s_last, carries = lax.fori_loop(0, qi, body, (s_diag, init))
        j_last = jnp.maximum(qi - 1, 0)
        carries = tuple(update(h, j_last, s, c) for h, s, c in zip(heads, s_last, carries))
        outs = [(acc / acc[ONES_LANE:ONES_LANE + 1, :]).T[:, :V_A] for _, acc in carries]
        o_ref[:, g0 * V_A:(g0 + FLASH_HEADS) * V_A] = jnp.concatenate(outs, axis=1).astype(BF16)


def _flash(q, k, vt, b, l):
    tq = min(ATTN_TQ, l)
    nq = l // tq
    n = H_A * HEAD_PAD
    return pl.pallas_call(
        _flash_kernel,
        grid=(b, nq),
        in_specs=[pl.BlockSpec((tq, n), lambda i, j: (i * nq + j, 0)),
                  pl.BlockSpec((l, n), lambda i, j: (i, 0)),
                  pl.BlockSpec((nq, n, tq), lambda i, j: (i, 0, 0))],
        out_specs=pl.BlockSpec((tq, D_MLA), lambda i, j: (i * nq + j, 0)),
        out_shape=jax.ShapeDtypeStruct((b * l, D_MLA), BF16),
        compiler_params=_cparams(("parallel", "arbitrary")),
        name="flash",
    )(q, k, vt)


def _matmul_kernel(x_ref, w_ref, o_ref):
    o_ref[...] = _dot(x_ref[...].astype(BF16), w_ref[...]).astype(o_ref.dtype)


def _matmul(x, w, dtype, name):
    t, kd = x.shape
    n = w.shape[1]
    return pl.pallas_call(
        _matmul_kernel,
        grid=(1,),
        in_specs=[pl.BlockSpec((t, kd), lambda i: (0, 0)), pl.BlockSpec((kd, n), lambda i: (0, 0))],
        out_specs=pl.BlockSpec((t, n), lambda i: (0, 0)),
        out_shape=jax.ShapeDtypeStruct((t, n), dtype),
        compiler_params=_cparams(("arbitrary",)),
        name=name,
    )(x, w)


def _decode_kernel(pt_ref, ql_ref, qp_ref, cn_ref, kn_ref, ckv_hbm, kpet_hbm, o_ref,
                   cbuf, kbuf, csem, ksem, *, layer, n_steps, n_chunks, g_pages, n_seq):
    step = pl.program_id(0)
    total = n_steps * n_chunks
    f0 = step * n_chunks

    def copies(f):
        st, c, slot = f // n_chunks, f % n_chunks, f % DECODE_SLOTS
        out = []
        for e in range(n_seq):
            for g in range(g_pages):
                page = pt_ref[st * n_seq + e, c * g_pages + g]
                dst = pl.ds(g * PAGE_SIZE, PAGE_SIZE)
                out.append(pltpu.make_async_copy(ckv_hbm.at[layer, page], cbuf.at[slot, e, dst], csem.at[slot]))
                out.append(pltpu.make_async_copy(kpet_hbm.at[layer, page], kbuf.at[slot, e, :, dst], ksem.at[slot]))
        return out

    def start(f):
        @pl.when(f < total)
        def _():
            for cp in copies(f):
                cp.start()

    def wait(f):
        for cp in copies(f):
            cp.wait()

    @pl.when(step == 0)
    def _():
        start(0)
        start(1)

    qls = [ql_ref[e].astype(BF16) for e in range(n_seq)]
    qps = [qp_ref[e].astype(BF16) for e in range(n_seq)]

    def scores(f):
        slot = f % DECODE_SLOTS
        return tuple((_dot_nt(qls[e], cbuf[slot, e].astype(BF16)) + _dot(qps[e], kbuf[slot, e].astype(BF16)))
                     * ATTN_SCALE for e in range(n_seq))

    def update(f, ss, carries):
        slot = f % DECODE_SLOTS
        out = []
        for e, (s, (m, l, acc)) in enumerate(zip(ss, carries)):
            m_new = jnp.maximum(m, jnp.max(s, axis=1, keepdims=True))
            p = jnp.exp(s - m_new)
            alpha = jnp.exp(m - m_new)
            l = alpha * l + jnp.sum(p, axis=1, keepdims=True)
            acc = alpha * acc + _dot(p.astype(BF16), cbuf[slot, e].astype(BF16))
            out.append((m_new, l, acc))
        return tuple(out)

    def chunk(c, state):
        ss, carries = state
        f = f0 + c
        start(f + 2)
        wait(f + 1)
        return scores(f + 1), update(f, ss, carries)

    init = tuple((jnp.full((H_A, 1), NEG, F32), jnp.zeros((H_A, 1), F32), jnp.zeros((H_A, KV_LORA), F32))
                 for _ in range(n_seq))
    wait(f0)
    ss, carries = lax.fori_loop(0, n_chunks - 1, chunk, (scores(f0), init))
    f_last = f0 + n_chunks - 1
    start(f_last + 2)
    carries = update(f_last, ss, carries)

    for e, (m, l, acc) in enumerate(carries):
        cn = cn_ref[e].astype(BF16).astype(F32)
        kn = kn_ref[e].astype(BF16).astype(F32)
        s_new = (jnp.sum(qls[e].astype(F32) * cn, axis=1, keepdims=True)
                 + jnp.sum(qps[e].astype(F32) * kn, axis=1, keepdims=True)) * ATTN_SCALE
        m_f = jnp.maximum(m, s_new)
        p_new = jnp.exp(s_new - m_f)
        alpha = jnp.exp(m - m_f)
        l = alpha * l + p_new
        acc = alpha * acc + p_new.astype(BF16).astype(F32) * cn
        o_ref[e] = acc / l


def _decode(page_table, ql, qp, cn, kn, cache_ckv, cache_kpet, layer):
    bd, n_pages = page_table.shape
    g_pages = min(DECODE_PAGES, n_pages)
    n_chunks = n_pages // g_pages
    n_seq = min(DECODE_SEQS, bd)
    rows = g_pages * PAGE_SIZE
    i3 = lambda i, pt: (i, 0, 0)
    grid_spec = pltpu.PrefetchScalarGridSpec(
        num_scalar_prefetch=1,
        grid=(bd // n_seq,),
        in_specs=[pl.BlockSpec((n_seq, H_A, KV_LORA), i3), pl.BlockSpec((n_seq, H_A, ROPE_DIM), i3),
                  pl.BlockSpec((n_seq, 1, KV_LORA), i3), pl.BlockSpec((n_seq, 1, ROPE_DIM), i3),
                  pl.BlockSpec(memory_space=pl.ANY), pl.BlockSpec(memory_space=pl.ANY)],
        out_specs=pl.BlockSpec((n_seq, H_A, KV_LORA), i3),
        scratch_shapes=[pltpu.VMEM((DECODE_SLOTS, n_seq, rows, KV_LORA), F32),
                        pltpu.VMEM((DECODE_SLOTS, n_seq, ROPE_DIM, rows), F32),
                        pltpu.SemaphoreType.DMA((DECODE_SLOTS,)), pltpu.SemaphoreType.DMA((DECODE_SLOTS,))],
    )
    return pl.pallas_call(
        functools.partial(_decode_kernel, layer=layer, n_steps=bd // n_seq, n_chunks=n_chunks,
                          g_pages=g_pages, n_seq=n_seq),
        grid_spec=grid_spec,
        out_shape=jax.ShapeDtypeStruct((bd, H_A, KV_LORA), F32),
        compiler_params=_cparams(("arbitrary",)),
        name="decode",
    )(page_table, ql, qp, cn, kn, cache_ckv, cache_kpet)


def _pool_kernel(u_ref, w_ref, sc_ref, y_ref):
    u = u_ref[0]
    row = lax.broadcasted_iota(jnp.int32, u.shape, 0)
    lane = lax.broadcasted_iota(jnp.int32, u.shape, 1)
    cnt = (row + 1).astype(F32)
    acc = u
    mean = jnp.zeros_like(u)
    k = 1
    for gi, w in enumerate(POOL_WINDOWS):
        while k < w:
            acc = acc + _shift_rows(acc, k, row)
            k *= 2
        in_group = (lane >= gi * D_POOL_G) & (lane < (gi + 1) * D_POOL_G)
        mean = jnp.where(in_group, acc / jnp.minimum(cnt, float(w)), mean)
    d = (mean - u).astype(BF16)
    y_ref[0] = (_dot(d, w_ref[...]) * sc_ref[...]).astype(BF16)


def _pool(u, w, sc):
    b, l, _ = u.shape
    i3 = lambda i: (i, 0, 0)
    return pl.pallas_call(
        _pool_kernel,
        grid=(b,),
        in_specs=[pl.BlockSpec((1, l, D_POOL), i3), pl.BlockSpec((D_POOL, D_POOL), lambda i: (0, 0)),
                  pl.BlockSpec((1, D_POOL), lambda i: (0, 0))],
        out_specs=pl.BlockSpec((1, l, D_POOL), i3),
        out_shape=jax.ShapeDtypeStruct((b, l, D_POOL), BF16),
        compiler_params=_cparams(("parallel",)),
        name="pool",
    )(u, w, sc)


def _outproj_kernel(x_ref, hm_ref, oa_ref, yp_ref, z_ref, w_ref, pw_ref, o_ref):
    z = z_ref[...].astype(F32)
    g = _silu(z)
    ym = (hm_ref[...].astype(F32) * g[:, :D_MLSTM]).astype(BF16)
    ya = (oa_ref[...].astype(F32) * g[:, D_MLSTM:D_MLSTM + D_MLA]).astype(BF16)
    yp = (yp_ref[...].astype(F32) * g[:, D_MLSTM + D_MLA:]).astype(BF16)
    mix = (_dot(ym, w_ref[:D_MLSTM, :]) + _dot(ya, w_ref[D_MLSTM:D_MLSTM + D_MLA, :])
           + _dot(yp, w_ref[D_MLSTM + D_MLA:, :]))
    o_ref[...] = x_ref[...] + _rms(mix, pw_ref[...])


def _outproj(x, hm, oa, yp, z, w, pw, tm):
    t = x.shape[0]
    row = lambda i: (i, 0)
    const = lambda i: (0, 0)
    return pl.pallas_call(
        _outproj_kernel,
        grid=(t // tm,),
        in_specs=[pl.BlockSpec((tm, D_MODEL), row), pl.BlockSpec((tm, D_MLSTM), row),
                  pl.BlockSpec((tm, D_MLA), row), pl.BlockSpec((tm, D_POOL), row),
                  pl.BlockSpec((tm, D_MODEL), row), pl.BlockSpec((D_MODEL, D_MODEL), const),
                  pl.BlockSpec((1, D_MODEL), const)],
        out_specs=pl.BlockSpec((tm, D_MODEL), row),
        out_shape=jax.ShapeDtypeStruct((t, D_MODEL), F32),
        compiler_params=_cparams(("parallel",)),
        name="outproj",
    )(x, hm, oa, yp, z, w, pw)


def _rope_tables(pos, off, scale=1.0):
    inv = ROPE_THETA ** (-jnp.arange(ROPE_HALF, dtype=F32) / ROPE_HALF)
    ang = pos.astype(F32)[:, None] * inv[None, :]
    cos, sin = jnp.cos(ang), jnp.sin(ang)
    n = pos.shape[0]
    pad_l = lambda fill: jnp.full((n, off), fill, F32)
    pad_r = jnp.zeros((n, LANES - off - ROPE_DIM), F32)
    c = jnp.concatenate([pad_l(1.0), cos, cos, pad_r], axis=1) * scale
    s = jnp.concatenate([pad_l(0.0), -sin, sin, pad_r], axis=1) * scale
    return c, s


def _layer_weights(w_in, gate_b, w_uq, w_uk, w_uv, pool_w, w_out):
    o = np.cumsum([0, 256, 256, 256, 4, 4, 256, 256, 384, 256, 32, 512, 256, 256])
    col = lambda i: w_in[:, o[i]:o[i + 1]]
    q_m, k_m, v_m, ig, fg, og, z_m, cq, ckv, kpe, z_a, u_p, z_p = [col(i) for i in range(13)]
    zc = lambda n: jnp.zeros((D_MODEL, n), F32)
    w_all = jnp.concatenate([
        q_m, k_m * (DK_M ** -0.5), v_m, og, z_m, z_a, z_p, cq, ckv, u_p,
        kpe, kpe, ig, fg, zc(LANES - GATE_OFF - 2 * H_M)], axis=1).astype(BF16)
    gb = jnp.concatenate([jnp.zeros((GATE_OFF,), F32), gate_b, jnp.zeros((LANES - GATE_OFF - 2 * H_M,), F32)])[None]

    wq = jnp.concatenate([w_uq, w_uq[:, :, NOPE:]], axis=2).reshape(Q_LORA, H_A * HEAD_PAD).astype(BF16)
    wk = jnp.concatenate([w_uk, jnp.zeros((KV_LORA, H_A, HEAD_PAD - NOPE), F32)], axis=2)
    wk = wk.reshape(KV_LORA, H_A * HEAD_PAD).astype(BF16)
    place = jnp.concatenate([jnp.zeros((ROPE_DIM, PE_OFF), F32), jnp.eye(ROPE_DIM, dtype=F32),
                             jnp.zeros((ROPE_DIM, HEAD_PAD - PE_OFF - ROPE_DIM), F32)], axis=1)
    wpe = jnp.tile(place, (1, H_A)).astype(BF16)
    wv = jnp.concatenate([w_uv, jnp.zeros((KV_LORA, H_A, HEAD_PAD - V_A), F32)], axis=2)
    wvt = wv.reshape(KV_LORA, H_A * HEAD_PAD).T.astype(BF16)
    vb = jnp.tile((jnp.arange(HEAD_PAD) == ONES_LANE).astype(F32), H_A)[:, None]

    eye_h = jnp.eye(H_A, dtype=F32)
    wabs = jnp.concatenate([jnp.transpose(w_uk, (1, 2, 0)),
                            jnp.zeros((H_A, HEAD_PAD - NOPE, KV_LORA), F32)], axis=1)
    wabs = jnp.einsum('hdc,hg->hdgc', wabs, eye_h).reshape(H_A * HEAD_PAD, H_A * KV_LORA).astype(BF16)
    wuv_bd = jnp.einsum('chd,hg->hcgd', w_uv, eye_h).reshape(H_A * KV_LORA, D_MLA).astype(BF16)
    eye_g = jnp.eye(len(POOL_WINDOWS), dtype=F32)
    wpool = jnp.einsum('gcd,gk->gckd', pool_w, eye_g).reshape(D_POOL, D_POOL).astype(BF16)
    return dict(w_all=w_all, gb=gb, wq=wq, wk=wk, wpe=wpe, wvt=wvt, vb=vb, wabs=wabs, wuv_bd=wuv_bd,
                wpool=wpool, w_out=w_out.astype(BF16))


def _row_tile(t):
    return min(512, t)


def kernel(x_prompt, x_sample, state_mlstm_C, state_mlstm_n, state_mlstm_m, cache_ckv, cache_kpe, state_pool, page_table, norm_pre_w, norm_post_w, w_in, mlstm_gate_b, mlstm_norm_w, mla_q_norm_w, mla_kv_norm_w, mla_w_uq, mla_w_uk, mla_w_uv, pool_w, pool_scale, w_out):
    b, l, _ = x_prompt.shape
    bd = x_sample.shape[0]
    depth = w_in.shape[0]
    n_past = page_table.shape[1] * PAGE_SIZE
    tp = b * l
    tm_p, tm_s = _row_tile(min(tp, l)), _row_tile(bd)

    pos_p = jnp.arange(l)
    pos_s = jnp.full((tm_s,), n_past)
    tabs_in_p, tabs_q_p = _rope_tables(pos_p, 0), _rope_tables(pos_p, PE_OFF, ATTN_SCALE * LOG2E)
    tabs_in_s, tabs_q_s = _rope_tables(pos_s, 0), _rope_tables(pos_s, PE_OFF)

    cache_kpet = jnp.swapaxes(cache_kpe, 2, 3)
    yp = x_prompt.reshape(tp, D_MODEL)
    ys = x_sample.reshape(bd, D_MODEL)
    outs = [[] for _ in range(12)]
    for li in range(depth):
        lw = _layer_weights(w_in[li], mlstm_gate_b[li], mla_w_uq[li], mla_w_uk[li], mla_w_uv[li],
                            pool_w[li], w_out[li])
        prew, postw = norm_pre_w[li][None], norm_post_w[li][None]
        qnw, kvnw, mnw = mla_q_norm_w[li][None], mla_kv_norm_w[li][None], mlstm_norm_w[li][None]
        psc = pool_scale[li][None]

        m, z, cq, ckv, u, kpe, g = _inproj(yp, prew, lw['w_all'], qnw, kvnw, lw['gb'], tabs_in_p, tm_p)
        hm, c_f, n_f, m_f = _mlstm_prompt(m, g, mnw, b, l)
        q = _qproj(cq, lw['wq'], tabs_q_p, tm_p)
        kk, vt = _kvproj(ckv, kpe, lw['wk'], lw['wpe'], lw['wvt'], lw['vb'], tm_p, min(ATTN_TQ, l))
        oa = _flash(q, kk, vt, b, l)
        u3 = u.reshape(b, l, D_POOL)
        ypool = _pool(u3, lw['wpool'], psc).reshape(tp, D_POOL)
        yp = _outproj(yp, hm, oa, ypool, z, lw['w_out'], postw, tm_p)
        outs[0].append(c_f)
        outs[1].append(n_f)
        outs[2].append(m_f[:, :, 0])
        outs[3].append(ckv.reshape(b, l, KV_LORA))
        outs[4].append(kpe.reshape(b, l, ROPE_DIM))
        outs[5].append(u3[:, l - POOL_BUF:])

        m, z, cq, ckv, u, kpe, g = _inproj(ys, prew, lw['w_all'], qnw, kvnw, lw['gb'], tabs_in_s, tm_s)
        hm, c_n, n_n, m_n = _mlstm_stepT(m, g, mnw, state_mlstm_C[li], state_mlstm_n[li], state_mlstm_m[li])
        q = _qproj(cq, lw['wq'], tabs_q_s, tm_s)
        ql = _matmul(q, lw['wabs'], F32, "absorb").reshape(bd, H_A, KV_LORA)
        qp = q.reshape(bd, H_A, HEAD_PAD)[:, :, PE_OFF:PE_OFF + ROPE_DIM]
        o_lat = _decode(page_table, ql, qp, ckv.reshape(bd, 1, KV_LORA), kpe.reshape(bd, 1, ROPE_DIM),
                        cache_ckv, cache_kpet, li)
        oa = _matmul(o_lat.reshape(bd, H_A * KV_LORA), lw['wuv_bd'], BF16, "uv")
        ext = jnp.concatenate([state_pool[li], u[:, None, :]], axis=1)
        ypool = _pool(ext, lw['wpool'], psc)[:, POOL_BUF]
        ys = _outproj(ys, hm.reshape(bd, D_MLSTM), oa, ypool, z, lw['w_out'], postw, tm_s)
        outs[6].append(c_n)
        outs[7].append(n_n)
        outs[8].append(m_n.reshape(bd, H_M))
        outs[9].append(ckv.reshape(bd, 1, KV_LORA))
        outs[10].append(kpe.reshape(bd, 1, ROPE_DIM))
        outs[11].append(ext[:, 1:])

    return (yp.reshape(b, l, D_MODEL), ys.reshape(bd, 1, D_MODEL)) + tuple(jnp.stack(o) for o in outs)
```

```python
import functools

import jax
import jax.numpy as jnp
import numpy as np
from jax import lax
from jax.experimental import pallas as pl
from jax.experimental.pallas import tpu as pltpu

F32 = jnp.float32
BF16 = jnp.bfloat16

D_MODEL = 1024
H_M = 4
D_MLSTM = 256
DK_M = 64
H_A = 8
D_MLA = 512
V_A = 64
NOPE = 64
ROPE_DIM = 32
ROPE_HALF = ROPE_DIM // 2
Q_LORA = 384
KV_LORA = 256
ROPE_THETA = 10000.0
ATTN_SCALE = (NOPE + ROPE_DIM) ** -0.5
D_POOL = 256
D_POOL_G = 64
POOL_WINDOWS = (2, 4, 8, 16)
POOL_BUF = 15
PAGE_SIZE = 128
RMS_EPS = 1e-6
NEG = -1e30

LANES = 128
HEAD_PAD = LANES
PE_OFF = NOPE
ONES_LANE = V_A
V_ROWS = 80
VMEM_LIMIT = 56 * 1024 * 1024

MLSTM_CHUNK = 256
ATTN_TQ = 256
FLASH_HEADS = 8
LOG2E = 1.4426950408889634
DECODE_PAGES = 16
DECODE_SEQS = 4
DECODE_SLOTS = 3


def _cparams(sem):
    return pltpu.CompilerParams(dimension_semantics=sem, vmem_limit_bytes=VMEM_LIMIT)


def _rms(x, w):
    return x * lax.rsqrt(jnp.mean(x * x, axis=-1, keepdims=True) + RMS_EPS) * w


def _silu(z):
    return z * jax.nn.sigmoid(z)


def _rope_lanes(x, c, s):
    return x * c + pltpu.roll(x, LANES - ROPE_HALF, 1) * s


def _shift_rows(x, k, row):
    return jnp.where(row >= k, pltpu.roll(x, k, 0), 0.0)


def _dot(a, b):
    return jnp.dot(a, b, preferred_element_type=F32)


def _dot_nt(a, b):
    return lax.dot_general(a, b, (((1,), (1,)), ((), ())), preferred_element_type=F32)


def _dot_tn(a, b):
    return lax.dot_general(a, b, (((0,), (0,)), ((), ())), preferred_element_type=F32)


C_M = 0
C_Z = 1024
C_CQ = 2048
C_CKV = 2432
C_U = 2688
C_X = 2944
N_IN = 3072
GATE_OFF = 64


def _inproj_kernel(x_ref, prew_ref, w_ref, qnw_ref, kvnw_ref, gb_ref, c_ref, s_ref,
                   m_ref, z_ref, cq_ref, ckv_ref, u_ref, kpe_ref, g_ref):
    x = x_ref[...]
    h = _rms(x, prew_ref[...]).astype(BF16)
    m_ref[...] = _dot(h, w_ref[:, C_M:C_Z]).astype(BF16)
    z_ref[...] = _dot(h, w_ref[:, C_Z:C_CQ]).astype(BF16)
    cq_ref[...] = _rms(_dot(h, w_ref[:, C_CQ:C_CKV]), qnw_ref[...]).astype(BF16)
    ckv_ref[...] = _rms(_dot(h, w_ref[:, C_CKV:C_U]), kvnw_ref[...])
    u_ref[...] = _dot(h, w_ref[:, C_U:C_X])
    xg = _dot(h, w_ref[:, C_X:N_IN])
    kpe_ref[...] = _rope_lanes(xg, c_ref[...], s_ref[...])[:, :ROPE_DIM]
    pre = xg + gb_ref[...]
    lane = lax.broadcasted_iota(jnp.int32, pre.shape, 1)
    logsig = jnp.minimum(pre, 0.0) - jnp.log1p(jnp.exp(-jnp.abs(pre)))
    g_ref[...] = jnp.where(lane < GATE_OFF + H_M, pre, logsig)


def _inproj(x, prew, w, qnw, kvnw, gb, tabs, tm):
    t = x.shape[0]
    n_tab = tabs[0].shape[0] // tm
    row = lambda i: (i, 0)
    const = lambda i: (0, 0)
    tab = lambda i: (i % n_tab, 0)
    outs = [
        jax.ShapeDtypeStruct((t, 1024), BF16), jax.ShapeDtypeStruct((t, 1024), BF16),
        jax.ShapeDtypeStruct((t, Q_LORA), BF16), jax.ShapeDtypeStruct((t, KV_LORA), F32),
        jax.ShapeDtypeStruct((t, D_POOL), F32), jax.ShapeDtypeStruct((t, ROPE_DIM), F32),
        jax.ShapeDtypeStruct((t, LANES), F32),
    ]
    return pl.pallas_call(
        _inproj_kernel,
        grid=(t // tm,),
        in_specs=[
            pl.BlockSpec((tm, D_MODEL), row), pl.BlockSpec((1, D_MODEL), const),
            pl.BlockSpec((D_MODEL, N_IN), const), pl.BlockSpec((1, Q_LORA), const),
            pl.BlockSpec((1, KV_LORA), const), pl.BlockSpec((1, LANES), const),
            pl.BlockSpec((tm, LANES), tab), pl.BlockSpec((tm, LANES), tab),
        ],
        out_specs=[pl.BlockSpec((tm, o.shape[1]), row) for o in outs],
        out_shape=outs,
        compiler_params=_cparams(("parallel",)),
        name="inproj",
    )(x, prew, w, qnw, kvnw, gb, *tabs)


def _mlstm_kernel(m_ref, g_ref, nw_ref, h_ref, c_out, n_out, m_out, ct_s, m_s):
    ci = pl.program_id(1)
    lc = m_ref.shape[0]

    @pl.when(ci == 0)
    def _():
        ct_s[...] = jnp.zeros_like(ct_s)
        m_s[...] = jnp.zeros_like(m_s)

    g = g_ref[...]
    lf = pltpu.roll(g, LANES - H_M, 1)
    tril = jnp.where(lax.broadcasted_iota(jnp.int32, (lc, lc), 0) >= lax.broadcasted_iota(jnp.int32, (lc, lc), 1),
                     1.0, 0.0).astype(BF16)
    hi = lf.astype(BF16)
    r1 = lf - hi.astype(F32)
    mid = r1.astype(BF16)
    lo = (r1 - mid.astype(F32)).astype(BF16)
    bcum = _dot(tril, hi) + _dot(tril, mid) + _dot(tril, lo)
    a = g - bcum
    gt = g.T
    bt = bcum.T
    causal = lax.broadcasted_iota(jnp.int32, (lc, lc), 0) <= lax.broadcasted_iota(jnp.int32, (lc, lc), 1)
    lane = lax.broadcasted_iota(jnp.int32, (lc, LANES), 1)
    low_rows = lax.broadcasted_iota(jnp.int32, (LANES, lc), 0) < DK_M

    for p in range(H_M // 2):
        qp = m_ref[:, p * LANES:(p + 1) * LANES]
        kp = m_ref[:, D_MLSTM + p * LANES:D_MLSTM + (p + 1) * LANES]
        vp = m_ref[:, 2 * D_MLSTM + p * LANES:2 * D_MLSTM + (p + 1) * LANES]
        ogt = m_ref[:, 3 * D_MLSTM + p * LANES:3 * D_MLSTM + (p + 1) * LANES].astype(F32).T
        halves = []
        for e in range(2):
            h = 2 * p + e
            ones_at = DK_M * (1 - e)
            own = jnp.where((lane >= DK_M * e) & (lane < DK_M * (e + 1)), 1.0, 0.0).astype(BF16)
            k_h = kp * own
            v_ext = vp * own + jnp.where(lane == ones_at, 1.0, 0.0).astype(BF16)
            r = GATE_OFF + h
            li_r, b_r, a_c = gt[r:r + 1, :], bt[r:r + 1, :], a[:, r:r + 1]
            m_prev = m_s[h:h + 1, 0:1]
            ct_prev = ct_s[h]

            d = jnp.where(causal, b_r + a_c, NEG)
            inter = b_r + m_prev
            m_t = jnp.maximum(inter, jnp.max(d, axis=0, keepdims=True))
            st = _dot_nt(k_h, qp) * jnp.exp(d - m_t)
            tot = jnp.exp(inter - m_t) * _dot_nt(ct_prev.astype(BF16), qp) + _dot_tn(v_ext, st.astype(BF16))
            den = tot[ones_at:ones_at + 1, :]
            halves.append(tot * (1.0 / jnp.maximum(jnp.abs(den), jnp.exp(-m_t))))

            b_last = b_r[:, lc - 1:lc]
            m_new = jnp.maximum(b_last + m_prev, jnp.max(b_last + li_r - b_r, axis=1, keepdims=True))
            wk = (k_h.astype(F32) * jnp.exp(b_last + a_c - m_new)).astype(BF16)
            ct_s[h] = jnp.exp(b_last + m_prev - m_new) * ct_prev + _dot_tn(v_ext, wk)
            m_s[h:h + 1, :] = jnp.broadcast_to(m_new, (1, LANES))

        hh = jnp.where(low_rows, halves[0], halves[1]) * jax.nn.sigmoid(ogt)
        sq = hh * hh
        inv = jnp.where(low_rows,
                        lax.rsqrt(jnp.mean(sq[:DK_M], axis=0, keepdims=True) + RMS_EPS),
                        lax.rsqrt(jnp.mean(sq[DK_M:], axis=0, keepdims=True) + RMS_EPS))
        y = hh * inv * nw_ref[p * LANES:(p + 1) * LANES, :]
        h_ref[:, p * LANES:(p + 1) * LANES] = y.T.astype(BF16)

    @pl.when(ci == pl.num_programs(1) - 1)
    def _():
        for h in range(H_M):
            e = h % 2
            sl = slice(DK_M * e, DK_M * (e + 1))
            ones_at = DK_M * (1 - e)
            c_out[0, h] = ct_s[h].T[sl, sl]
            n_out[0, h:h + 1, :] = ct_s[h][ones_at:ones_at + 1, sl]
        m_out[0] = m_s[0:H_M, :]


def _mlstm_prompt(m, g, nw, b, l):
    lc = min(MLSTM_CHUNK, l)
    nc = l // lc
    row = lambda i, j: (i * nc + j, 0)
    outs = [
        jax.ShapeDtypeStruct((b * l, D_MLSTM), BF16),
        jax.ShapeDtypeStruct((b, H_M, DK_M, DK_M), F32),
        jax.ShapeDtypeStruct((b, H_M, DK_M), F32),
        jax.ShapeDtypeStruct((b, H_M, LANES), F32),
    ]
    nw_rep = jnp.broadcast_to(nw.reshape(D_MLSTM, 1), (D_MLSTM, lc))
    return pl.pallas_call(
        _mlstm_kernel,
        grid=(b, nc),
        in_specs=[
            pl.BlockSpec((lc, 1024), row), pl.BlockSpec((lc, LANES), row),
            pl.BlockSpec((D_MLSTM, lc), lambda i, j: (0, 0)),
        ],
        out_specs=[
            pl.BlockSpec((lc, D_MLSTM), row),
            pl.BlockSpec((1, H_M, DK_M, DK_M), lambda i, j: (i, 0, 0, 0)),
            pl.BlockSpec((1, H_M, DK_M), lambda i, j: (i, 0, 0)),
            pl.BlockSpec((1, H_M, LANES), lambda i, j: (i, 0, 0)),
        ],
        out_shape=outs,
        scratch_shapes=[pltpu.VMEM((H_M, LANES, LANES), F32), pltpu.VMEM((8, LANES), F32)],
        compiler_params=_cparams(("parallel", "arbitrary")),
        name="mlstm_prompt",
    )(m, g, nw_rep)


def _mlstm_stepT_kernel(q_ref, k_ref, v_ref, og_ref, n_ref, gm_ref, nw_ref, c_ref,
                        h_ref, c_out, n_out, m_out):
    to_t = lambda ref: ref[...].astype(F32).T
    qt, kt, vt, ogt, nt = to_t(q_ref), to_t(k_ref), to_t(v_ref), to_t(og_ref), to_t(n_ref)
    gm = gm_ref[0]
    ys, ns = [], []
    for e in range(2):
        sl = slice(DK_M * e, DK_M * (e + 1))
        q, k, v, n_prev = qt[sl], kt[sl], vt[sl], nt[sl]
        li, lf, m0 = gm[e:e + 1], gm[2 + e:3 + e], gm[4 + e:5 + e]
        inter = lf + m0
        m_t = jnp.maximum(inter, li)
        w_intra = jnp.exp(li - m_t)
        w_inter = jnp.exp(inter - m_t)
        s = jnp.sum(q * k, axis=0, keepdims=True) * w_intra
        wk = k * w_intra
        qc = jnp.zeros_like(v)
        for d in range(DK_M):
            c_d = c_ref[e, d]
            qc = qc + q[d:d + 1] * c_d
            c_out[e, d] = w_inter * c_d + wk[d:d + 1] * v
        num = w_inter * qc + s * v
        den = w_inter * jnp.sum(q * n_prev, axis=0, keepdims=True) + s
        hh = num / jnp.maximum(jnp.abs(den), jnp.exp(-m_t)) * jax.nn.sigmoid(ogt[sl])
        ys.append(hh * lax.rsqrt(jnp.mean(hh * hh, axis=0, keepdims=True) + RMS_EPS))
        ns.append(w_inter * n_prev + wk)
        m_out[0, e:e + 1, :] = m_t
    h_ref[...] = (jnp.concatenate(ys, axis=0) * nw_ref[...]).T.astype(BF16)
    n_out[...] = jnp.concatenate(ns, axis=0).T


def _mlstm_stepT(m, g, nw, c, n, mi):
    bd = m.shape[0]
    n_pair = H_M // 2
    ct = jnp.transpose(c, (1, 2, 3, 0))
    gt = g[:, GATE_OFF:GATE_OFF + 2 * H_M].T
    rows = [jnp.stack([gt[2 * p], gt[2 * p + 1], gt[H_M + 2 * p], gt[H_M + 2 * p + 1],
                       mi[:, 2 * p], mi[:, 2 * p + 1], jnp.zeros((bd,), F32), jnp.zeros((bd,), F32)])
            for p in range(n_pair)]
    gm = jnp.stack(rows)
    nw_rep = jnp.broadcast_to(nw.reshape(D_MLSTM, 1), (D_MLSTM, bd))
    col = lambda off: (lambda p: (0, off + p))
    outs = [
        jax.ShapeDtypeStruct((bd, D_MLSTM), BF16),
        jax.ShapeDtypeStruct((H_M, DK_M, DK_M, bd), F32),
        jax.ShapeDtypeStruct((bd, D_MLSTM), F32),
        jax.ShapeDtypeStruct((n_pair, 2, bd), F32),
    ]
    h, ct_new, n_new, m_new = pl.pallas_call(
        _mlstm_stepT_kernel,
        grid=(n_pair,),
        in_specs=[
            pl.BlockSpec((bd, LANES), col(0)), pl.BlockSpec((bd, LANES), col(2)),
            pl.BlockSpec((bd, LANES), col(4)), pl.BlockSpec((bd, LANES), col(6)),
            pl.BlockSpec((bd, LANES), col(0)), pl.BlockSpec((1, 8, bd), lambda p: (p, 0, 0)),
            pl.BlockSpec((LANES, bd), lambda p: (p, 0)),
            pl.BlockSpec((2, DK_M, DK_M, bd), lambda p: (p, 0, 0, 0)),
        ],
        out_specs=[
            pl.BlockSpec((bd, LANES), col(0)),
            pl.BlockSpec((2, DK_M, DK_M, bd), lambda p: (p, 0, 0, 0)),
            pl.BlockSpec((bd, LANES), col(0)),
            pl.BlockSpec((1, 2, bd), lambda p: (p, 0, 0)),
        ],
        out_shape=outs,
        compiler_params=_cparams(("parallel",)),
        name="mlstm_step",
    )(m, m, m, m, n.reshape(bd, D_MLSTM), gm, nw_rep, ct)
    return (h, jnp.transpose(ct_new, (3, 0, 1, 2)), n_new.reshape(bd, H_M, DK_M),
            m_new.reshape(H_M, bd).T)


def _qproj_kernel(cq_ref, w_ref, c_ref, s_ref, q_ref):
    q = _dot(cq_ref[...], w_ref[...])
    c, s = c_ref[...], s_ref[...]
    for h in range(H_A):
        sl = slice(h * HEAD_PAD, (h + 1) * HEAD_PAD)
        q_ref[:, sl] = _rope_lanes(q[:, sl], c, s).astype(BF16)


def _qproj(cq, w, tabs, tm):
    t = cq.shape[0]
    n_tab = tabs[0].shape[0] // tm
    row = lambda i: (i, 0)
    tab = lambda i: (i % n_tab, 0)
    return pl.pallas_call(
        _qproj_kernel,
        grid=(t // tm,),
        in_specs=[pl.BlockSpec((tm, Q_LORA), row), pl.BlockSpec((Q_LORA, H_A * HEAD_PAD), lambda i: (0, 0)),
                  pl.BlockSpec((tm, LANES), tab), pl.BlockSpec((tm, LANES), tab)],
        out_specs=pl.BlockSpec((tm, H_A * HEAD_PAD), row),
        out_shape=jax.ShapeDtypeStruct((t, H_A * HEAD_PAD), BF16),
        compiler_params=_cparams(("parallel",)),
        name="qproj",
    )(cq, w, *tabs)


def _kvproj_kernel(ckv_ref, kpe_ref, wk_ref, wpe_ref, wvt_ref, vb_ref, k_ref, vt_ref):
    ckv = ckv_ref[...].astype(BF16)
    k_ref[...] = (_dot(ckv, wk_ref[...]) + _dot(kpe_ref[...].astype(BF16), wpe_ref[...])).astype(BF16)
    tk = vt_ref.shape[2]
    for i in range(vt_ref.shape[0]):
        vt_ref[i] = (_dot_nt(wvt_ref[...], ckv[i * tk:(i + 1) * tk, :]) + vb_ref[...]).astype(BF16)


def _kvproj(ckv, kpe, wk, wpe, wvt, vb, tm, tk):
    t = ckv.shape[0]
    row = lambda i: (i, 0)
    const = lambda i: (0, 0)
    n, nv = H_A * HEAD_PAD, H_A * V_ROWS
    return pl.pallas_call(
        _kvproj_kernel,
        grid=(t // tm,),
        in_specs=[pl.BlockSpec((tm, KV_LORA), row), pl.BlockSpec((tm, ROPE_DIM), row),
                  pl.BlockSpec((KV_LORA, n), const), pl.BlockSpec((ROPE_DIM, n), const),
                  pl.BlockSpec((nv, KV_LORA), const), pl.BlockSpec((nv, 1), const)],
        out_specs=[pl.BlockSpec((tm, n), row), pl.BlockSpec((tm // tk, nv, tk), lambda i: (i, 0, 0))],
        out_shape=[jax.ShapeDtypeStruct((t, n), BF16), jax.ShapeDtypeStruct((t // tk, nv, tk), BF16)],
        compiler_params=_cparams(("parallel",)),
        name="kvproj",
    )(ckv, kpe, wk, wpe, wvt, vb)


def _flash_kernel(q_ref, k_ref, vt_ref, o_ref):
    qi = pl.program_id(1)
    tq = q_ref.shape[0]
    tk = vt_ref.shape[2]
    key = lax.broadcasted_iota(jnp.int32, (tk, tq), 0)
    qry = lax.broadcasted_iota(jnp.int32, (tk, tq), 1)
    causal = key <= qry

    def scores(h, j):
        sl = slice(h * HEAD_PAD, (h + 1) * HEAD_PAD)
        off = pl.multiple_of(j * tk, tk)
        return _dot_nt(k_ref[pl.ds(off, tk), sl], q_ref[:, sl])

    def update(h, j, s, carry):
        m, acc = carry
        m_new = jnp.maximum(m, jnp.max(s, axis=0, keepdims=True))
        p = jnp.exp2(s - m_new)
        acc = jnp.exp2(m - m_new) * acc + _dot(vt_ref[j, h * V_ROWS:(h + 1) * V_ROWS, :], p.astype(BF16))
        return m_new, acc

    for g0 in range(0, H_A, FLASH_HEADS):
        heads = range(g0, g0 + FLASH_HEADS)

        def body(j, state):
            s_cur, carries = state
            j_cur = jnp.where(j == 0, qi, j - 1)
            s_next = tuple(scores(h, j) for h in heads)
            carries = tuple(update(h, j_cur, s, c) for h, s, c in zip(heads, s_cur, carries))
            return s_next, carries

        s_diag = tuple(jnp.where(causal, scores(h, qi), NEG) for h in heads)
        init = tuple((jnp.full((1, tq), NEG, F32), jnp.zeros((V_ROWS, tq), F32)) for _ in heads)
        s_last, carries = lax.fori_loop(0, qi, body, (s_diag, init))
        j_last = jnp.maximum(qi - 1, 0)
        carries = tuple(update(h, j_last, s, c) for h, s, c in zip(heads, s_last, carries))
        fill = jnp.zeros((LANES - V_ROWS, tq), F32)
        outs = [jnp.concatenate([acc / acc[ONES_LANE:ONES_LANE + 1, :], fill], axis=0).T[:, :V_A]
                for _, acc in carries]
        o_ref[:, g0 * V_A:(g0 + FLASH_HEADS) * V_A] = jnp.concatenate(outs, axis=1).astype(BF16)


def _flash(q, k, vt, b, l):
    tq = min(ATTN_TQ, l)
    nq = l // tq
    n = H_A * HEAD_PAD
    return pl.pallas_call(
        _flash_kernel,
        grid=(b, nq),
        in_specs=[pl.BlockSpec((tq, n), lambda i, j: (i * nq + j, 0)),
                  pl.BlockSpec((l, n), lambda i, j: (i, 0)),
                  pl.BlockSpec((nq, H_A * V_ROWS, tq), lambda i, j: (i, 0, 0))],
        out_specs=pl.BlockSpec((tq, D_MLA), lambda i, j: (i * nq + j, 0)),
        out_shape=jax.ShapeDtypeStruct((b * l, D_MLA), BF16),
        compiler_params=_cparams(("parallel", "arbitrary")),
        name="flash",
    )(q, k, vt)


def _matmul_kernel(x_ref, w_ref, o_ref):
    o_ref[...] = _dot(x_ref[...].astype(BF16), w_ref[...]).astype(o_ref.dtype)


def _matmul(x, w, dtype, name):
    t, kd = x.shape
    n = w.shape[1]
    return pl.pallas_call(
        _matmul_kernel,
        grid=(1,),
        in_specs=[pl.BlockSpec((t, kd), lambda i: (0, 0)), pl.BlockSpec((kd, n), lambda i: (0, 0))],
        out_specs=pl.BlockSpec((t, n), lambda i: (0, 0)),
        out_shape=jax.ShapeDtypeStruct((t, n), dtype),
        compiler_params=_cparams(("arbitrary",)),
        name=name,
    )(x, w)


def _decode_kernel(pt_ref, ql_ref, qp_ref, cn_ref, kn_ref, ckv_hbm, kpet_hbm, o_ref,
                   cbuf, kbuf, csem, ksem, *, layer, n_steps, n_chunks, g_pages, n_seq):
    step = pl.program_id(0)
    total = n_steps * n_chunks
    f0 = step * n_chunks

    def copies(f):
        st, c, slot = f // n_chunks, f % n_chunks, f % DECODE_SLOTS
        out = []
        for e in range(n_seq):
            for g in range(g_pages):
                page = pt_ref[st * n_seq + e, c * g_pages + g]
                dst = pl.ds(g * PAGE_SIZE, PAGE_SIZE)
                out.append(pltpu.make_async_copy(ckv_hbm.at[layer, page], cbuf.at[slot, e, dst], csem.at[slot]))
                out.append(pltpu.make_async_copy(kpet_hbm.at[layer, page], kbuf.at[slot, e, :, dst], ksem.at[slot]))
        return out

    def start(f):
        @pl.when(f < total)
        def _():
            for cp in copies(f):
                cp.start()

    def wait(f):
        for cp in copies(f):
            cp.wait()

    @pl.when(step == 0)
    def _():
        start(0)
        start(1)

    qls = [ql_ref[e].astype(BF16) for e in range(n_seq)]
    qps = [qp_ref[e].astype(BF16) for e in range(n_seq)]

    def scores(f):
        slot = f % DECODE_SLOTS
        return tuple((_dot_nt(qls[e], cbuf[slot, e].astype(BF16)) + _dot(qps[e], kbuf[slot, e].astype(BF16)))
                     * ATTN_SCALE for e in range(n_seq))

    def update(f, ss, carries):
        slot = f % DECODE_SLOTS
        out = []
        for e, (s, (m, l, acc)) in enumerate(zip(ss, carries)):
            m_new = jnp.maximum(m, jnp.max(s, axis=1, keepdims=True))
            p = jnp.exp(s - m_new)
            alpha = jnp.exp(m - m_new)
            l = alpha * l + jnp.sum(p, axis=1, keepdims=True)
            acc = alpha * acc + _dot(p.astype(BF16), cbuf[slot, e].astype(BF16))
            out.append((m_new, l, acc))
        return tuple(out)

    def chunk(c, state):
        ss, carries = state
        f = f0 + c
        start(f + 2)
        wait(f + 1)
        return scores(f + 1), update(f, ss, carries)

    init = tuple((jnp.full((H_A, 1), NEG, F32), jnp.zeros((H_A, 1), F32), jnp.zeros((H_A, KV_LORA), F32))
                 for _ in range(n_seq))
    wait(f0)
    ss, carries = lax.fori_loop(0, n_chunks - 1, chunk, (scores(f0), init))
    f_last = f0 + n_chunks - 1
    start(f_last + 2)
    carries = update(f_last, ss, carries)

    for e, (m, l, acc) in enumerate(carries):
        cn = cn_ref[e].astype(BF16).astype(F32)
        kn = kn_ref[e].astype(BF16).astype(F32)
        s_new = (jnp.sum(qls[e].astype(F32) * cn, axis=1, keepdims=True)
                 + jnp.sum(qps[e].astype(F32) * kn, axis=1, keepdims=True)) * ATTN_SCALE
        m_f = jnp.maximum(m, s_new)
        p_new = jnp.exp(s_new - m_f)
        alpha = jnp.exp(m - m_f)
        l = alpha * l + p_new
        acc = alpha * acc + p_new.astype(BF16).astype(F32) * cn
        o_ref[e] = acc / l


def _decode(page_table, ql, qp, cn, kn, cache_ckv, cache_kpet, layer):
    bd, n_pages = page_table.shape
    g_pages = min(DECODE_PAGES, n_pages)
    n_chunks = n_pages // g_pages
    n_seq = min(DECODE_SEQS, bd)
    rows = g_pages * PAGE_SIZE
    i3 = lambda i, pt: (i, 0, 0)
    grid_spec = pltpu.PrefetchScalarGridSpec(
        num_scalar_prefetch=1,
        grid=(bd // n_seq,),
        in_specs=[pl.BlockSpec((n_seq, H_A, KV_LORA), i3), pl.BlockSpec((n_seq, H_A, ROPE_DIM), i3),
                  pl.BlockSpec((n_seq, 1, KV_LORA), i3), pl.BlockSpec((n_seq, 1, ROPE_DIM), i3),
                  pl.BlockSpec(memory_space=pl.ANY), pl.BlockSpec(memory_space=pl.ANY)],
        out_specs=pl.BlockSpec((n_seq, H_A, KV_LORA), i3),
        scratch_shapes=[pltpu.VMEM((DECODE_SLOTS, n_seq, rows, KV_LORA), F32),
                        pltpu.VMEM((DECODE_SLOTS, n_seq, ROPE_DIM, rows), F32),
                        pltpu.SemaphoreType.DMA((DECODE_SLOTS,)), pltpu.SemaphoreType.DMA((DECODE_SLOTS,))],
    )
    return pl.pallas_call(
        functools.partial(_decode_kernel, layer=layer, n_steps=bd // n_seq, n_chunks=n_chunks,
                          g_pages=g_pages, n_seq=n_seq),
        grid_spec=grid_spec,
        out_shape=jax.ShapeDtypeStruct((bd, H_A, KV_LORA), F32),
        compiler_params=_cparams(("arbitrary",)),
        name="decode",
    )(page_table, ql, qp, cn, kn, cache_ckv, cache_kpet)


def _pool_kernel(u_ref, w_ref, sc_ref, y_ref):
    u = u_ref[0]
    row = lax.broadcasted_iota(jnp.int32, u.shape, 0)
    lane = lax.broadcasted_iota(jnp.int32, u.shape, 1)
    cnt = (row + 1).astype(F32)
    acc = u
    mean = jnp.zeros_like(u)
    k = 1
    for gi, w in enumerate(POOL_WINDOWS):
        while k < w:
            acc = acc + _shift_rows(acc, k, row)
            k *= 2
        in_group = (lane >= gi * D_POOL_G) & (lane < (gi + 1) * D_POOL_G)
        mean = jnp.where(in_group, acc / jnp.minimum(cnt, float(w)), mean)
    d = (mean - u).astype(BF16)
    y_ref[0] = (_dot(d, w_ref[...]) * sc_ref[...]).astype(BF16)


def _pool(u, w, sc):
    b, l, _ = u.shape
    i3 = lambda i: (i, 0, 0)
    return pl.pallas_call(
        _pool_kernel,
        grid=(b,),
        in_specs=[pl.BlockSpec((1, l, D_POOL), i3), pl.BlockSpec((D_POOL, D_POOL), lambda i: (0, 0)),
                  pl.BlockSpec((1, D_POOL), lambda i: (0, 0))],
        out_specs=pl.BlockSpec((1, l, D_POOL), i3),
        out_shape=jax.ShapeDtypeStruct((b, l, D_POOL), BF16),
        compiler_params=_cparams(("parallel",)),
        name="pool",
    )(u, w, sc)


def _outproj_kernel(x_ref, hm_ref, oa_ref, yp_ref, z_ref, w_ref, pw_ref, o_ref):
    z = z_ref[...].astype(F32)
    g = _silu(z)
    ym = (hm_ref[...].astype(F32) * g[:, :D_MLSTM]).astype(BF16)
    ya = (oa_ref[...].astype(F32) * g[:, D_MLSTM:D_MLSTM + D_MLA]).astype(BF16)
    yp = (yp_ref[...].astype(F32) * g[:, D_MLSTM + D_MLA:]).astype(BF16)
    mix = (_dot(ym, w_ref[:D_MLSTM, :]) + _dot(ya, w_ref[D_MLSTM:D_MLSTM + D_MLA, :])
           + _dot(yp, w_ref[D_MLSTM + D_MLA:, :]))
    o_ref[...] = x_ref[...] + _rms(mix, pw_ref[...])


def _outproj(x, hm, oa, yp, z, w, pw, tm):
    t = x.shape[0]
    row = lambda i: (i, 0)
    const = lambda i: (0, 0)
    return pl.pallas_call(
        _outproj_kernel,
        grid=(t // tm,),
        in_specs=[pl.BlockSpec((tm, D_MODEL), row), pl.BlockSpec((tm, D_MLSTM), row),
                  pl.BlockSpec((tm, D_MLA), row), pl.BlockSpec((tm, D_POOL), row),
                  pl.BlockSpec((tm, D_MODEL), row), pl.BlockSpec((D_MODEL, D_MODEL), const),
                  pl.BlockSpec((1, D_MODEL), const)],
        out_specs=pl.BlockSpec((tm, D_MODEL), row),
        out_shape=jax.ShapeDtypeStruct((t, D_MODEL), F32),
        compiler_params=_cparams(("parallel",)),
        name="outproj",
    )(x, hm, oa, yp, z, w, pw)


def _rope_tables(pos, off, scale=1.0):
    inv = ROPE_THETA ** (-jnp.arange(ROPE_HALF, dtype=F32) / ROPE_HALF)
    ang = pos.astype(F32)[:, None] * inv[None, :]
    cos, sin = jnp.cos(ang), jnp.sin(ang)
    n = pos.shape[0]
    pad_l = lambda fill: jnp.full((n, off), fill, F32)
    pad_r = jnp.zeros((n, LANES - off - ROPE_DIM), F32)
    c = jnp.concatenate([pad_l(1.0), cos, cos, pad_r], axis=1) * scale
    s = jnp.concatenate([pad_l(0.0), -sin, sin, pad_r], axis=1) * scale
    return c, s


def _layer_weights(w_in, gate_b, w_uq, w_uk, w_uv, pool_w, w_out):
    o = np.cumsum([0, 256, 256, 256, 4, 4, 256, 256, 384, 256, 32, 512, 256, 256])
    col = lambda i: w_in[:, o[i]:o[i + 1]]
    q_m, k_m, v_m, ig, fg, og, z_m, cq, ckv, kpe, z_a, u_p, z_p = [col(i) for i in range(13)]
    zc = lambda n: jnp.zeros((D_MODEL, n), F32)
    w_all = jnp.concatenate([
        q_m, k_m * (DK_M ** -0.5), v_m, og, z_m, z_a, z_p, cq, ckv, u_p,
        kpe, kpe, ig, fg, zc(LANES - GATE_OFF - 2 * H_M)], axis=1).astype(BF16)
    gb = jnp.concatenate([jnp.zeros((GATE_OFF,), F32), gate_b, jnp.zeros((LANES - GATE_OFF - 2 * H_M,), F32)])[None]

    wq = jnp.concatenate([w_uq, w_uq[:, :, NOPE:]], axis=2).reshape(Q_LORA, H_A * HEAD_PAD).astype(BF16)
    wk = jnp.concatenate([w_uk, jnp.zeros((KV_LORA, H_A, HEAD_PAD - NOPE), F32)], axis=2)
    wk = wk.reshape(KV_LORA, H_A * HEAD_PAD).astype(BF16)
    place = jnp.concatenate([jnp.zeros((ROPE_DIM, PE_OFF), F32), jnp.eye(ROPE_DIM, dtype=F32),
                             jnp.zeros((ROPE_DIM, HEAD_PAD - PE_OFF - ROPE_DIM), F32)], axis=1)
    wpe = jnp.tile(place, (1, H_A)).astype(BF16)
    wv = jnp.concatenate([w_uv, jnp.zeros((KV_LORA, H_A, V_ROWS - V_A), F32)], axis=2)
    wvt = wv.reshape(KV_LORA, H_A * V_ROWS).T.astype(BF16)
    vb = jnp.tile((jnp.arange(V_ROWS) == ONES_LANE).astype(F32), H_A)[:, None]

    eye_h = jnp.eye(H_A, dtype=F32)
    wabs = jnp.concatenate([jnp.transpose(w_uk, (1, 2, 0)),
                            jnp.zeros((H_A, HEAD_PAD - NOPE, KV_LORA), F32)], axis=1)
    wabs = jnp.einsum('hdc,hg->hdgc', wabs, eye_h).reshape(H_A * HEAD_PAD, H_A * KV_LORA).astype(BF16)
    wuv_bd = jnp.einsum('chd,hg->hcgd', w_uv, eye_h).reshape(H_A * KV_LORA, D_MLA).astype(BF16)
    eye_g = jnp.eye(len(POOL_WINDOWS), dtype=F32)
    wpool = jnp.einsum('gcd,gk->gckd', pool_w, eye_g).reshape(D_POOL, D_POOL).astype(BF16)
    return dict(w_all=w_all, gb=gb, wq=wq, wk=wk, wpe=wpe, wvt=wvt, vb=vb, wabs=wabs, wuv_bd=wuv_bd,
                wpool=wpool, w_out=w_out.astype(BF16))


def _row_tile(t):
    return min(512, t)


def kernel(x_prompt, x_sample, state_mlstm_C, state_mlstm_n, state_mlstm_m, cache_ckv, cache_kpe, state_pool, page_table, norm_pre_w, norm_post_w, w_in, mlstm_gate_b, mlstm_norm_w, mla_q_norm_w, mla_kv_norm_w, mla_w_uq, mla_w_uk, mla_w_uv, pool_w, pool_scale, w_out):
    b, l, _ = x_prompt.shape
    bd = x_sample.shape[0]
    depth = w_in.shape[0]
    n_past = page_table.shape[1] * PAGE_SIZE
    tp = b * l
    tm_p, tm_s = _row_tile(min(tp, l)), _row_tile(bd)

    pos_p = jnp.arange(l)
    pos_s = jnp.full((tm_s,), n_past)
    tabs_in_p, tabs_q_p = _rope_tables(pos_p, 0), _rope_tables(pos_p, PE_OFF, ATTN_SCALE * LOG2E)
    tabs_in_s, tabs_q_s = _rope_tables(pos_s, 0), _rope_tables(pos_s, PE_OFF)

    cache_kpet = jnp.swapaxes(cache_kpe, 2, 3)
    yp = x_prompt.reshape(tp, D_MODEL)
    ys = x_sample.reshape(bd, D_MODEL)
    outs = [[] for _ in range(12)]
    for li in range(depth):
        lw = _layer_weights(w_in[li], mlstm_gate_b[li], mla_w_uq[li], mla_w_uk[li], mla_w_uv[li],
                            pool_w[li], w_out[li])
        prew, postw = norm_pre_w[li][None], norm_post_w[li][None]
        qnw, kvnw, mnw = mla_q_norm_w[li][None], mla_kv_norm_w[li][None], mlstm_norm_w[li][None]
        psc = pool_scale[li][None]

        m, z, cq, ckv, u, kpe, g = _inproj(yp, prew, lw['w_all'], qnw, kvnw, lw['gb'], tabs_in_p, tm_p)
        hm, c_f, n_f, m_f = _mlstm_prompt(m, g, mnw, b, l)
        q = _qproj(cq, lw['wq'], tabs_q_p, tm_p)
        kk, vt = _kvproj(ckv, kpe, lw['wk'], lw['wpe'], lw['wvt'], lw['vb'], tm_p, min(ATTN_TQ, l))
        oa = _flash(q, kk, vt, b, l)
        u3 = u.reshape(b, l, D_POOL)
        ypool = _pool(u3, lw['wpool'], psc).reshape(tp, D_POOL)
        yp = _outproj(yp, hm, oa, ypool, z, lw['w_out'], postw, tm_p)
        outs[0].append(c_f)
        outs[1].append(n_f)
        outs[2].append(m_f[:, :, 0])
        outs[3].append(ckv.reshape(b, l, KV_LORA))
        outs[4].append(kpe.reshape(b, l, ROPE_DIM))
        outs[5].append(u3[:, l - POOL_BUF:])

        m, z, cq, ckv, u, kpe, g = _inproj(ys, prew, lw['w_all'], qnw, kvnw, lw['gb'], tabs_in_s, tm_s)
        hm, c_n, n_n, m_n = _mlstm_stepT(m, g, mnw, state_mlstm_C[li], state_mlstm_n[li], state_mlstm_m[li])
        q = _qproj(cq, lw['wq'], tabs_q_s, tm_s)
        ql = _matmul(q, lw['wabs'], F32, "absorb").reshape(bd, H_A, KV_LORA)
        qp = q.reshape(bd, H_A, HEAD_PAD)[:, :, PE_OFF:PE_OFF + ROPE_DIM]
        o_lat = _decode(page_table, ql, qp, ckv.reshape(bd, 1, KV_LORA), kpe.reshape(bd, 1, ROPE_DIM),
                        cache_ckv, cache_kpet, li)
        oa = _matmul(o_lat.reshape(bd, H_A * KV_LORA), lw['wuv_bd'], BF16, "uv")
        ext = jnp.concatenate([state_pool[li], u[:, None, :]], axis=1)
        ypool = _pool(ext.reshape(1, bd * (POOL_BUF + 1), D_POOL), lw['wpool'], psc)
        ypool = ypool.reshape(bd, POOL_BUF + 1, D_POOL)[:, POOL_BUF]
        ys = _outproj(ys, hm.reshape(bd, D_MLSTM), oa, ypool, z, lw['w_out'], postw, tm_s)
        outs[6].append(c_n)
        outs[7].append(n_n)
        outs[8].append(m_n.reshape(bd, H_M))
        outs[9].append(ckv.reshape(bd, 1, KV_LORA))
        outs[10].append(kpe.reshape(bd, 1, ROPE_DIM))
        outs[11].append(ext[:, 1:])

    return (yp.reshape(b, l, D_MODEL), ys.reshape(bd, 1, D_MODEL)) + tuple(jnp.stack(o) for o in outs)
```
